```python
import math
import jax, jax.numpy as jnp
from jax import lax
import numpy as np

D_MODEL = 1024
BATCH = 16
SEQ = 2048
DEPTH = 4

N_MIXERS = 2
N_NSA_LAYERS = (DEPTH + N_MIXERS - 1) // N_MIXERS
N_SSD_LAYERS = DEPTH // N_MIXERS

NSA_HEADS = 16
NSA_HEAD_DIM = 64
NSA_KV_GROUPS = 4
NSA_HEADS_PER_GROUP = NSA_HEADS // NSA_KV_GROUPS
COMP_BLOCK = 32
COMP_STRIDE = 16
COMP_HIDDEN = 256
SEL_BLOCK = 64
N_SELECT = 16
N_LOCAL_BLOCKS = 2
WINDOW = 512
N_BRANCHES = 3
WIN_Q_BLOCK = 128
SEL_Q_BLOCK = 16
NSA_Q_DIM = NSA_HEADS * NSA_HEAD_DIM
NSA_KV_DIM = NSA_KV_GROUPS * NSA_HEAD_DIM
NSA_PROJ_DIM = NSA_Q_DIM + 2 * N_BRANCHES * NSA_KV_DIM + NSA_HEADS * N_BRANCHES

SSD_EXPAND = 2
SSD_D_INNER = SSD_EXPAND * D_MODEL
SSD_HEAD_DIM = 64
SSD_HEADS = SSD_D_INNER // SSD_HEAD_DIM
SSD_GROUPS = 4
SSD_STATE = 128
SSD_CONV = 4
SSD_CHUNK = 128
SSD_CONV_DIM = SSD_D_INNER + 2 * SSD_GROUPS * SSD_STATE
SSD_PROJ_DIM = SSD_D_INNER + SSD_CONV_DIM + SSD_HEADS

FFN_HIDDEN = 2816
FFN_CONV = 3

RMS_EPS = 1e-6
NEG_INF = -1e30
SEL_FORCE = 1e9

kernel_name = "hybrid_nsa_ssd_convffn_trunk"


def rms_norm(x, gain):
    xf = x.astype(jnp.float32)
    y = xf * lax.rsqrt(jnp.mean(xf * xf, axis=-1, keepdims=True) + RMS_EPS)
    return (y * gain.astype(jnp.float32)).astype(x.dtype)


def causal_dwconv(x, w, b):
    width, ch = w.shape
    y = lax.conv_general_dilated(
        x, w.astype(x.dtype)[:, None, :], window_strides=(1,),
        padding=[(width - 1, 0)], dimension_numbers=("NWC", "WIO", "NWC"),
        feature_group_count=ch)
    return y + b.astype(x.dtype)


def masked_softmax(scores, mask):
    return jax.nn.softmax(jnp.where(mask, scores, NEG_INF), axis=-1)


def compress_tokens(t, pos, w1, w2):
    b, s, g, d = t.shape
    nc = (s - COMP_BLOCK) // COMP_STRIDE + 1
    idx = np.arange(nc)[:, None] * COMP_STRIDE + np.arange(COMP_BLOCK)[None, :]
    blocks = t[:, idx] + pos.astype(t.dtype)[None, None, :, None, :]
    flat = blocks.transpose(0, 1, 3, 2, 4).reshape(b, nc, g, COMP_BLOCK * d)
    return jax.nn.gelu(flat @ w1) @ w2


def cmp_to_sel_overlap(nc, nsb):
    cs = np.arange(nc) * COMP_STRIDE
    ss = np.arange(nsb) * SEL_BLOCK
    ov = (np.minimum(cs[:, None] + COMP_BLOCK, ss[None, :] + SEL_BLOCK)
          - np.maximum(cs[:, None], ss[None, :]))
    return np.clip(ov, 0, None).astype(np.float32) / COMP_BLOCK


def nsa_mixer(h, w_in, cmp_pos, cmp_w1, cmp_w2, w_out):
    b, s, _ = h.shape
    g, hpg, hd = NSA_KV_GROUPS, NSA_HEADS_PER_GROUP, NSA_HEAD_DIM
    scale = hd ** -0.5
    kv_end = NSA_Q_DIM + 2 * N_BRANCHES * NSA_KV_DIM
    proj = h @ w_in
    q = proj[..., :NSA_Q_DIM].reshape(b, s, g, hpg, hd)
    kv = proj[..., NSA_Q_DIM:kv_end].reshape(b, s, 2 * N_BRANCHES, g, hd)
    gates = jax.nn.sigmoid(proj[..., kv_end:].astype(jnp.float32)).reshape(b, s, g, hpg, N_BRANCHES)
    k_cmp, v_cmp = kv[:, :, 0], kv[:, :, 1]
    k_sel, v_sel = kv[:, :, 2], kv[:, :, 3]
    k_win, v_win = kv[:, :, 4], kv[:, :, 5]
    pos = np.arange(s)

    kc = compress_tokens(k_cmp, cmp_pos[0], cmp_w1[0], cmp_w2[0])
    vc = compress_tokens(v_cmp, cmp_pos[1], cmp_w1[1], cmp_w2[1])
    nc = kc.shape[1]
    cmp_mask = (np.arange(nc)[None, :] * COMP_STRIDE + COMP_BLOCK - 1) <= pos[:, None]
    sc = jnp.einsum("bsghd,bcgd->bghsc", q, kc).astype(jnp.float32) * scale
    p_cmp = jnp.where(cmp_mask, masked_softmax(sc, cmp_mask), 0.0)
    o_cmp = jnp.einsum("bghsc,bcgd->bsghd", p_cmp.astype(vc.dtype), vc)

    nsb = s // SEL_BLOCK
    p_slc = jnp.einsum("bgsc,cj->bgsj", p_cmp.sum(axis=2),
                       jnp.asarray(cmp_to_sel_overlap(nc, nsb)))
    blk = np.arange(nsb)[None, :]
    cur = (pos // SEL_BLOCK)[:, None]
    allowed = blk * SEL_BLOCK <= pos[:, None]
    forced = (blk == 0) | ((blk <= cur) & (blk > cur - N_LOCAL_BLOCKS))
    sel_score = jnp.where(forced, SEL_FORCE, jnp.where(allowed, p_slc, NEG_INF))
    n_sel = min(N_SELECT, nsb)
    _, sel_idx = lax.top_k(sel_score, n_sel)
    ks_blocks = k_sel.reshape(b, nsb, SEL_BLOCK, g, hd).transpose(0, 3, 1, 2, 4).reshape(b, g, nsb, SEL_BLOCK * hd)
    vs_blocks = v_sel.reshape(b, nsb, SEL_BLOCK, g, hd).transpose(0, 3, 1, 2, 4).reshape(b, g, nsb, SEL_BLOCK * hd)
    n_qc = s // SEL_Q_BLOCK
    q_ch = q.reshape(b, n_qc, SEL_Q_BLOCK, g, hpg, hd).transpose(1, 0, 3, 4, 2, 5)
    idx_ch = sel_idx.reshape(b, g, n_qc, SEL_Q_BLOCK, n_sel).transpose(2, 0, 1, 3, 4)
    pos_ch = jnp.arange(s).reshape(n_qc, SEL_Q_BLOCK)
    n_keys = n_sel * SEL_BLOCK

    def sel_chunk(args):
        qc, ic, pc = args
        flat_idx = ic.reshape(b, g, SEL_Q_BLOCK * n_sel)[..., None]
        kg = jnp.take_along_axis(ks_blocks, flat_idx, axis=2).reshape(b, g, SEL_Q_BLOCK, n_keys, hd)
        vg = jnp.take_along_axis(vs_blocks, flat_idx, axis=2).reshape(b, g, SEL_Q_BLOCK, n_keys, hd)
        kpos = (ic[..., None] * SEL_BLOCK + jnp.arange(SEL_BLOCK)).reshape(b, g, SEL_Q_BLOCK, n_keys)
        mask = (kpos <= pc[None, None, :, None])[:, :, None]
        scs = jnp.einsum("bghqd,bgqkd->bghqk", qc, kg).astype(jnp.float32) * scale
        p = masked_softmax(scs, mask)
        return jnp.einsum("bghqk,bgqkd->bghqd", p.astype(vg.dtype), vg)

    o_sel = lax.map(sel_chunk, (q_ch, idx_ch, pos_ch)).transpose(1, 0, 4, 2, 3, 5).reshape(b, s, g, hpg, hd)

    n_wb = s // WIN_Q_BLOCK
    span = WIN_Q_BLOCK + WINDOW
    kp = jnp.pad(k_win, ((0, 0), (WINDOW, 0), (0, 0), (0, 0)))
    vp = jnp.pad(v_win, ((0, 0), (WINDOW, 0), (0, 0), (0, 0)))
    q_wb = q.reshape(b, n_wb, WIN_Q_BLOCK, g, hpg, hd).transpose(1, 0, 3, 4, 2, 5)

    def win_block(args):
        qb, i = args
        start = i * WIN_Q_BLOCK
        kb = lax.dynamic_slice_in_dim(kp, start, span, axis=1)
        vb = lax.dynamic_slice_in_dim(vp, start, span, axis=1)
        qpos = start + jnp.arange(WIN_Q_BLOCK)
        kpos = start - WINDOW + jnp.arange(span)
        mask = ((kpos[None, :] >= 0) & (kpos[None, :] <= qpos[:, None])
                & (kpos[None, :] > qpos[:, None] - WINDOW))
        scs = jnp.einsum("bghqd,bkgd->bghqk", qb, kb).astype(jnp.float32) * scale
        p = masked_softmax(scs, mask)
        return jnp.einsum("bghqk,bkgd->bghqd", p.astype(vb.dtype), vb)

    o_win = lax.map(win_block, (q_wb, jnp.arange(n_wb))).transpose(1, 0, 4, 2, 3, 5).reshape(b, s, g, hpg, hd)

    o = (gates[..., 0:1] * o_cmp + gates[..., 1:2] * o_sel + gates[..., 2:3] * o_win).astype(h.dtype)
    return o.reshape(b, s, NSA_Q_DIM) @ w_out


def ssd_scan(x, dt, a, bm, cm):
    b, s, nh, p = x.shape
    g, n = bm.shape[2], bm.shape[3]
    hg = nh // g
    L = SSD_CHUNK
    nc = s // L
    xc = x.reshape(b, nc, L, g, hg, p).transpose(1, 0, 2, 3, 4, 5)
    dtc = dt.reshape(b, nc, L, g, hg).transpose(1, 0, 2, 3, 4)
    bc = bm.reshape(b, nc, L, g, n).transpose(1, 0, 2, 3, 4)
    cc = cm.reshape(b, nc, L, g, n).transpose(1, 0, 2, 3, 4)
    a_g = a.reshape(g, hg)
    causal = np.tril(np.ones((L, L), dtype=bool))[None, :, :, None, None]

    def step(state, inp):
        x_k, dt_k, b_k, c_k = inp
        acs = jnp.cumsum(dt_k * a_g, axis=1)
        seg = acs[:, :, None] - acs[:, None, :]
        lmat = jnp.exp(jnp.where(causal, seg, -jnp.inf))
        cb = jnp.einsum("blgn,bsgn->blsg", c_k, b_k)
        w = cb[..., None] * lmat * dt_k[:, None]
        y = jnp.einsum("blsgh,bsghp->blghp", w, x_k)
        y = y + jnp.einsum("blgn,bghpn->blghp", c_k, state) * jnp.exp(acs)[..., None]
        decay = jnp.exp(acs[:, -1:] - acs) * dt_k
        state = (state * jnp.exp(acs[:, -1])[..., None, None]
                 + jnp.einsum("bsgn,bsgh,bsghp->bghpn", b_k, decay, x_k))
        return state, y

    state0 = jnp.zeros((b, g, hg, p, n), jnp.float32)
    _, ys = lax.scan(step, state0, (xc, dtc, bc, cc))
    return ys.transpose(1, 0, 2, 3, 4, 5).reshape(b, s, nh, p)


def ssd_mixer(h, w_in, conv_w, conv_b, dt_bias, a_log, d_skip, norm_w, w_out):
    b, s, _ = h.shape
    f32 = jnp.float32
    gn = SSD_GROUPS * SSD_STATE
    proj = h @ w_in
    z = proj[..., :SSD_D_INNER]
    xbc = jax.nn.silu(causal_dwconv(proj[..., SSD_D_INNER:SSD_D_INNER + SSD_CONV_DIM], conv_w, conv_b))
    dt_raw = proj[..., SSD_D_INNER + SSD_CONV_DIM:]
    x_ssm = xbc[..., :SSD_D_INNER].astype(f32).reshape(b, s, SSD_HEADS, SSD_HEAD_DIM)
    bm = xbc[..., SSD_D_INNER:SSD_D_INNER + gn].astype(f32).reshape(b, s, SSD_GROUPS, SSD_STATE)
    cm = xbc[..., SSD_D_INNER + gn:].astype(f32).reshape(b, s, SSD_GROUPS, SSD_STATE)
    dt = jax.nn.softplus(dt_raw.astype(f32) + dt_bias.astype(f32))
    a = -jnp.exp(a_log.astype(f32))
    y = ssd_scan(x_ssm, dt, a, bm, cm) + d_skip.astype(f32)[:, None] * x_ssm
    y = y.reshape(b, s, SSD_D_INNER) * jax.nn.silu(z.astype(f32))
    yg = y.reshape(b, s, SSD_GROUPS, SSD_D_INNER // SSD_GROUPS)
    yg = yg * lax.rsqrt(jnp.mean(yg * yg, axis=-1, keepdims=True) + RMS_EPS)
    y = (yg.reshape(b, s, SSD_D_INNER) * norm_w.astype(f32)).astype(h.dtype)
    return y @ w_out


def conv_ffn(h, w_up, conv_w, conv_b, w_down):
    u = causal_dwconv(h @ w_up, conv_w, conv_b)
    val, gate = u[..., :FFN_HIDDEN], u[..., FFN_HIDDEN:]
    return (jax.nn.silu(gate) * val) @ w_down


def setup_inputs(seed: int = 0) -> dict:
    key = jax.random.key(seed)
    ks = jax.random.split(key, 19)
    f32 = jnp.float32

    def dense(k, shape, fan_in):
        return jax.random.normal(k, shape, f32) * fan_in ** -0.5

    x = jax.random.normal(ks[0], (BATCH, SEQ, D_MODEL), f32)
    norm_gains = 1.0 + 0.05 * jax.random.normal(ks[1], (DEPTH, 4, D_MODEL), f32)
    nsa_w_in = dense(ks[2], (N_NSA_LAYERS, D_MODEL, NSA_PROJ_DIM), D_MODEL)
    nsa_cmp_pos = 0.1 * jax.random.normal(ks[3], (N_NSA_LAYERS, 2, COMP_BLOCK, NSA_HEAD_DIM), f32)
    nsa_cmp_w1 = dense(ks[4], (N_NSA_LAYERS, 2, COMP_BLOCK * NSA_HEAD_DIM, COMP_HIDDEN), COMP_BLOCK * NSA_HEAD_DIM)
    nsa_cmp_w2 = dense(ks[5], (N_NSA_LAYERS, 2, COMP_HIDDEN, NSA_HEAD_DIM), COMP_HIDDEN)
    nsa_w_out = dense(ks[6], (N_NSA_LAYERS, NSA_Q_DIM, D_MODEL), NSA_Q_DIM)
    ssd_w_in = dense(ks[7], (N_SSD_LAYERS, D_MODEL, SSD_PROJ_DIM), D_MODEL)
    ssd_conv_w = dense(ks[8], (N_SSD_LAYERS, SSD_CONV, SSD_CONV_DIM), SSD_CONV)
    ssd_conv_b = 0.02 * jax.random.normal(ks[9], (N_SSD_LAYERS, SSD_CONV_DIM), f32)
    dt0 = jnp.exp(jax.random.uniform(ks[10], (N_SSD_LAYERS, SSD_HEADS), f32,
                                     minval=math.log(1e-3), maxval=math.log(1e-1)))
    ssd_dt_bias = dt0 + jnp.log(-jnp.expm1(-dt0))
    ssd_a_log = jnp.log(jax.random.uniform(ks[11], (N_SSD_LAYERS, SSD_HEADS), f32, minval=1.0, maxval=16.0))
    ssd_d = 1.0 + 0.05 * jax.random.normal(ks[12], (N_SSD_LAYERS, SSD_HEADS), f32)
    ssd_norm_w = 1.0 + 0.05 * jax.random.normal(ks[13], (N_SSD_LAYERS, SSD_D_INNER), f32)
    ssd_w_out = dense(ks[14], (N_SSD_LAYERS, SSD_D_INNER, D_MODEL), SSD_D_INNER)
    ffn_w_up = dense(ks[15], (DEPTH, D_MODEL, 2 * FFN_HIDDEN), D_MODEL)
    ffn_conv_w = dense(ks[16], (DEPTH, FFN_CONV, 2 * FFN_HIDDEN), FFN_CONV)
    ffn_conv_b = 0.02 * jax.random.normal(ks[17], (DEPTH, 2 * FFN_HIDDEN), f32)
    ffn_w_down = dense(ks[18], (DEPTH, FFN_HIDDEN, D_MODEL), FFN_HIDDEN)
    return {
        "x": x, "norm_gains": norm_gains,
        "nsa_w_in": nsa_w_in, "nsa_cmp_pos": nsa_cmp_pos, "nsa_cmp_w1": nsa_cmp_w1,
        "nsa_cmp_w2": nsa_cmp_w2, "nsa_w_out": nsa_w_out,
        "ssd_w_in": ssd_w_in, "ssd_conv_w": ssd_conv_w, "ssd_conv_b": ssd_conv_b,
        "ssd_dt_bias": ssd_dt_bias, "ssd_a_log": ssd_a_log, "ssd_d": ssd_d,
        "ssd_norm_w": ssd_norm_w, "ssd_w_out": ssd_w_out,
        "ffn_w_up": ffn_w_up, "ffn_conv_w": ffn_conv_w, "ffn_conv_b": ffn_conv_b,
        "ffn_w_down": ffn_w_down,
    }


def reference(x, norm_gains, nsa_w_in, nsa_cmp_pos, nsa_cmp_w1, nsa_cmp_w2, nsa_w_out,
              ssd_w_in, ssd_conv_w, ssd_conv_b, ssd_dt_bias, ssd_a_log, ssd_d, ssd_norm_w,
              ssd_w_out, ffn_w_up, ffn_conv_w, ffn_conv_b, ffn_w_down):
    h = x
    for i in range(DEPTH):
        gains = norm_gains[i]
        slot = i // N_MIXERS
        hn = rms_norm(h, gains[0])
        if i % N_MIXERS == 0:
            m = nsa_mixer(hn, nsa_w_in[slot], nsa_cmp_pos[slot], nsa_cmp_w1[slot],
                          nsa_cmp_w2[slot], nsa_w_out[slot])
        else:
            m = ssd_mixer(hn, ssd_w_in[slot], ssd_conv_w[slot], ssd_conv_b[slot],
                          ssd_dt_bias[slot], ssd_a_log[slot], ssd_d[slot],
                          ssd_norm_w[slot], ssd_w_out[slot])
        h = h + rms_norm(m, gains[1])
        f = conv_ffn(rms_norm(h, gains[2]), ffn_w_up[i], ffn_conv_w[i], ffn_conv_b[i], ffn_w_down[i])
        h = h + rms_norm(f, gains[3])
    return h
```

```python
import functools

import numpy as np
import jax
import jax.numpy as jnp
from jax import lax
from jax.experimental import pallas as pl
from jax.experimental.pallas import tpu as pltpu

F32 = jnp.float32
BF16 = jnp.bfloat16

D_MODEL = 1024
DEPTH = 4
N_MIXERS = 2

NSA_HEADS = 16
NSA_HEAD_DIM = 64
NSA_KV_GROUPS = 4
NSA_HPG = NSA_HEADS // NSA_KV_GROUPS
COMP_BLOCK = 32
COMP_STRIDE = 16
COMP_HIDDEN = 256
SEL_BLOCK = 64
N_SELECT = 16
N_LOCAL_BLOCKS = 2
WINDOW = 512
N_BRANCHES = 3
NSA_Q_DIM = NSA_HEADS * NSA_HEAD_DIM
NSA_KV_DIM = NSA_KV_GROUPS * NSA_HEAD_DIM

SSD_D_INNER = 2 * D_MODEL
SSD_HEAD_DIM = 64
SSD_HEADS = SSD_D_INNER // SSD_HEAD_DIM
SSD_GROUPS = 4
SSD_STATE = 128
SSD_CONV = 4
SSD_CHUNK = 128
SSD_CONV_DIM = SSD_D_INNER + 2 * SSD_GROUPS * SSD_STATE
SSD_HEADS_PER_GROUP = SSD_HEADS // SSD_GROUPS
SSD_GROUP_WIDTH = SSD_D_INNER // SSD_GROUPS

FFN_HIDDEN = 2816
FFN_CONV = 3

RMS_EPS = 1e-6
NEG_INF = -1e30
SEL_FORCE = 1e9
MASK_BIG = 2.0 ** 100

LANES = 128
HALO = 16
VMEM_LIMIT = 56 * 1024 * 1024

PROJ_TM = 256
OUT_TM = 512
FFN_TM = 512
FFN_FC = 256
ATT_TQ = 256
ATT_TK = 256
CMP_ROWS = 128
CMP_SEQ_PER_STEP = 8


def _cparams(*sem):
    return pltpu.CompilerParams(dimension_semantics=sem, vmem_limit_bytes=VMEM_LIMIT)


def _resident(shape):
    nd = len(shape)
    return pl.BlockSpec(shape, lambda *_: (0,) * nd, pipeline_mode=pl.Buffered(1))


def _rms(x, gain):
    ms = jnp.mean(x * x, axis=-1, keepdims=True)
    return x * lax.rsqrt(ms + RMS_EPS) * gain


def _sigmoid(x):
    return 1.0 / (1.0 + jnp.exp(-x))


def _dot(a, b):
    return jnp.dot(a, b, preferred_element_type=F32)


def _dot_nt(a, b):
    return lax.dot_general(a, b, (((1,), (1,)), ((), ())), preferred_element_type=F32)


def _split_bf16(x, parts):
    out = []
    r = x
    for _ in range(parts):
        t = r.astype(BF16)
        out.append(t)
        r = r - t.astype(F32)
    return out


def _causal_conv(u_ext, cw, width):
    y = cw[width - 1:width, :] * u_ext[HALO:, :] + cw[width:width + 1, :]
    for back in range(1, width):
        shifted = pltpu.roll(u_ext, back, axis=0)[HALO:, :]
        y = y + cw[width - 1 - back:width - back, :] * shifted
    return y


def _fill_halo(xe_ref, xn, tm, first_of_seq):
    @pl.when(first_of_seq)
    def _():
        xe_ref[0:HALO, :] = jnp.zeros((HALO, xe_ref.shape[1]), xe_ref.dtype)

    @pl.when(jnp.logical_not(first_of_seq))
    def _():
        xe_ref[0:HALO, :] = xe_ref[tm:tm + HALO, :]

    xe_ref[HALO:, :] = xn


def _norm_proj_kernel(x_ref, g_ref, w_ref, *out_refs):
    xn = _rms(x_ref[...], g_ref[...]).astype(BF16)
    off = 0
    for o_ref in out_refs:
        n = o_ref.shape[1]
        o_ref[...] = _dot(xn, w_ref[:, off:off + n]).astype(o_ref.dtype)
        off += n


def _norm_proj(x, gain, w, outs):
    t, d = x.shape
    tm = PROJ_TM
    return pl.pallas_call(
        _norm_proj_kernel,
        grid=(t // tm,),
        in_specs=[pl.BlockSpec((tm, d), lambda i: (i, 0)),
                  _resident((1, d)),
                  _resident(w.shape)],
        out_specs=[pl.BlockSpec((tm, n), lambda i: (i, 0)) for n, _ in outs],
        out_shape=[jax.ShapeDtypeStruct((t, n), dt) for n, dt in outs],
        compiler_params=_cparams("parallel"),
        name="norm_proj",
    )(x, gain, w)


def _out_proj_kernel(y_ref, w_ref, g_ref, h_ref, o_ref):
    f = _dot(y_ref[...], w_ref[...])
    o_ref[...] = h_ref[...] + _rms(f, g_ref[...])


def _out_proj(y, w, gain, h):
    t, k = y.shape
    d = w.shape[1]
    tm = OUT_TM
    return pl.pallas_call(
        _out_proj_kernel,
        grid=(t // tm,),
        in_specs=[pl.BlockSpec((tm, k), lambda i: (i, 0)),
                  _resident(w.shape),
                  _resident((1, d)),
                  pl.BlockSpec((tm, d), lambda i: (i, 0))],
        out_specs=pl.BlockSpec((tm, d), lambda i: (i, 0)),
        out_shape=jax.ShapeDtypeStruct((t, d), F32),
        compiler_params=_cparams("parallel"),
        name="out_proj",
    )(y, w, gain, h)


def _ffn_kernel(x_ref, g1_ref, wv_ref, wg_ref, cv_ref, cg_ref, wd_ref, g2_ref, o_ref, xe_ref,
                *, tiles_per_seq):
    tm = x_ref.shape[0]
    x = x_ref[...]
    xn = _rms(x, g1_ref[...]).astype(BF16)
    _fill_halo(xe_ref, xn, tm, pl.program_id(0) % tiles_per_seq == 0)
    xe = xe_ref[...]

    def chunk(c, acc):
        val = _causal_conv(_dot(xe, wv_ref[c]), cv_ref[c], FFN_CONV)
        gate = _causal_conv(_dot(xe, wg_ref[c]), cg_ref[c], FFN_CONV)
        act = (gate * _sigmoid(gate) * val).astype(BF16)
        return acc + _dot(act, wd_ref[c])

    f = lax.fori_loop(0, wv_ref.shape[0], chunk, jnp.zeros((tm, D_MODEL), F32))
    o_ref[...] = x + _rms(f, g2_ref[...])


def _ffn(x, g1, wv, wg, cv, cg, wd, g2, seq_len):
    t, d = x.shape
    tm = FFN_TM
    kern = functools.partial(_ffn_kernel, tiles_per_seq=seq_len // tm)
    return pl.pallas_call(
        kern,
        grid=(t // tm,),
        in_specs=[pl.BlockSpec((tm, d), lambda i: (i, 0)),
                  _resident((1, d)),
                  _resident(wv.shape), _resident(wg.shape),
                  _resident(cv.shape), _resident(cg.shape),
                  _resident(wd.shape),
                  _resident((1, d))],
        out_specs=pl.BlockSpec((tm, d), lambda i: (i, 0)),
        out_shape=jax.ShapeDtypeStruct((t, d), F32),
        scratch_shapes=[pltpu.VMEM((tm + HALO, d), BF16)],
        compiler_params=_cparams("arbitrary"),
        name="conv_ffn",
    )(x, g1, wv, wg, cv, cg, wd, g2)


def _prep_ffn_weights(w_up, conv_w, conv_b, w_down):
    f, fc = FFN_HIDDEN, FFN_FC
    nch = f // fc

    def chunk_cols(w):
        return w.reshape(w.shape[0], nch, fc).transpose(1, 0, 2)

    def conv_tab(cw, cb):
        tab = jnp.concatenate([cw, cb[None, :], jnp.zeros((8 - FFN_CONV - 1, f), F32)], axis=0)
        return chunk_cols(tab)

    wv = chunk_cols(w_up[:, :f]).astype(BF16)
    wg = chunk_cols(w_up[:, f:]).astype(BF16)
    cv = conv_tab(conv_w[:, :f], conv_b[:f])
    cg = conv_tab(conv_w[:, f:], conv_b[f:])
    wd = w_down.reshape(nch, fc, D_MODEL).astype(BF16)
    return wv, wg, cv, cg, wd


def _gelu_tanh(x):
    return 0.5 * x * (1.0 + jnp.tanh(float(np.sqrt(2.0 / np.pi)) * (x + 0.044715 * (x * x * x))))


def _compress_kernel(a_ref, pos_ref, w1a_ref, w1b_ref, w2_ref, o_ref):
    a = a_ref[0]
    m = a.shape[0]
    u1 = _dot((a + pos_ref[0, 0]).astype(BF16), w1a_ref[0])
    u2 = _dot((a + pos_ref[0, 1]).astype(BF16), w1b_ref[0])
    hid = u1 + pltpu.roll(u2, m - 1, axis=0)
    o_ref[0] = _dot(_gelu_tanh(hid).astype(BF16), w2_ref[0])


def _compress(a, pos, w1a, w1b, w2):
    _, r, k = a.shape
    m = CMP_SEQ_PER_STEP * CMP_ROWS
    return pl.pallas_call(
        _compress_kernel,
        grid=(2, r // m),
        in_specs=[pl.BlockSpec((1, m, k), lambda kv, i: (kv, i, 0)),
                  pl.BlockSpec((1, 2, 1, k), lambda kv, i: (kv, 0, 0, 0)),
                  pl.BlockSpec((1, k, COMP_HIDDEN), lambda kv, i: (kv, 0, 0)),
                  pl.BlockSpec((1, k, COMP_HIDDEN), lambda kv, i: (kv, 0, 0)),
                  pl.BlockSpec((1, COMP_HIDDEN, NSA_HEAD_DIM), lambda kv, i: (kv, 0, 0))],
        out_specs=pl.BlockSpec((1, m, NSA_HEAD_DIM), lambda kv, i: (kv, i, 0)),
        out_shape=jax.ShapeDtypeStruct((2, r, NSA_HEAD_DIM), F32),
        compiler_params=_cparams("parallel", "parallel"),
        name="nsa_compress",
    )(a, pos, w1a, w1b, w2)


def _cmp_sel_kernel(q_ref, kc_ref, vc_ref, ov_ref, place_ref, ocmp_ref, qa_ref):
    tq = q_ref.shape[0]
    q0 = pl.program_id(2) * tq
    kc = kc_ref[0]
    vc = vc_ref[0]
    nsb = ov_ref.shape[0]

    c_io = lax.broadcasted_iota(jnp.int32, (CMP_ROWS, tq), 0)
    pos_c = q0 + lax.broadcasted_iota(jnp.int32, (CMP_ROWS, tq), 1)
    cmask = c_io * COMP_STRIDE + (COMP_BLOCK - 1) <= pos_c

    psum = jnp.zeros((CMP_ROWS, tq), F32)
    for h in range(NSA_HPG):
        qh = q_ref[:, h * LANES:(h + 1) * LANES]
        st = jnp.where(cmask, _dot_nt(kc, qh), NEG_INF)
        mx = jnp.max(st, axis=0, keepdims=True)
        e = jnp.where(cmask, jnp.exp(st - mx), 0.0)
        den = jnp.sum(e, axis=0, keepdims=True)
        p = e / jnp.where(den > 0.0, den, 1.0)
        psum = psum + p
        ocmp_ref[0, 0, h] = _dot(p.T.astype(BF16), vc)

    hi, lo = _split_bf16(psum, 2)
    pslc = _dot(ov_ref[...], hi) + _dot(ov_ref[...], lo)

    j_io = lax.broadcasted_iota(jnp.int32, (nsb, tq), 0)
    pos_j = q0 + lax.broadcasted_iota(jnp.int32, (nsb, tq), 1)
    cur = jnp.right_shift(pos_j, SEL_BLOCK.bit_length() - 1)
    allowed = j_io * SEL_BLOCK <= pos_j
    forced = (j_io == 0) | ((j_io <= cur) & (j_io > cur - N_LOCAL_BLOCKS))
    score = jnp.where(forced, SEL_FORCE, jnp.where(allowed, pslc, NEG_INF))

    rank = jnp.zeros((nsb, tq), F32)
    for i in range(nsb):
        row = score[i:i + 1, :]
        before = (row > score) | ((row == score) & (j_io > i))
        rank = rank + jnp.where(before, 1.0, 0.0)
    bias = jnp.where(rank < float(min(N_SELECT, nsb)), 0.0, -MASK_BIG)
    bias = jnp.concatenate([bias, jnp.zeros((LANES - nsb, tq), F32)], axis=0)
    placed = _dot(bias.T.astype(BF16), place_ref[...])
    for h in range(NSA_HPG):
        sl = slice(h * LANES, (h + 1) * LANES)
        qa_ref[:, sl] = (q_ref[:, sl].astype(F32) + placed).astype(BF16)


def _cmp_sel(qpad, kcp, vc, ov_t, place, batch, seq_len):
    t = qpad.shape[0]
    tq = ATT_TQ
    nq = seq_len // tq
    g, hpg = NSA_KV_GROUPS, NSA_HPG
    gw = hpg * LANES
    return pl.pallas_call(
        _cmp_sel_kernel,
        grid=(batch, g, nq),
        in_specs=[pl.BlockSpec((tq, gw), lambda b, gi, i: (b * nq + i, gi)),
                  pl.BlockSpec((1, CMP_ROWS, LANES), lambda b, gi, i: (b * g + gi, 0, 0)),
                  pl.BlockSpec((1, CMP_ROWS, NSA_HEAD_DIM), lambda b, gi, i: (b * g + gi, 0, 0)),
                  pl.BlockSpec(ov_t.shape, lambda b, gi, i: (0, 0)),
                  pl.BlockSpec(place.shape, lambda b, gi, i: (0, 0))],
        out_specs=[pl.BlockSpec((1, 1, hpg, tq, NSA_HEAD_DIM), lambda b, gi, i: (b, gi, 0, i, 0)),
                   pl.BlockSpec((tq, gw), lambda b, gi, i: (b * nq + i, gi))],
        out_shape=[jax.ShapeDtypeStruct((batch, g, hpg, seq_len, NSA_HEAD_DIM), F32),
                   jax.ShapeDtypeStruct((t, g * gw), BF16)],
        compiler_params=_cparams("parallel", "parallel", "parallel"),
        name="nsa_cmp_select",
    )(qpad, kcp, vc, ov_t, place)


def _flash_step(q, k, v, mask, carry):
    m, l, acc = carry
    s = _dot_nt(q, k)
    if mask is not None:
        s = jnp.where(mask, s, NEG_INF)
    m_new = jnp.maximum(m, jnp.max(s, axis=-1, keepdims=True))
    alpha = jnp.exp(m - m_new)
    p = jnp.exp(s - m_new)
    l = alpha * l + jnp.sum(p, axis=-1, keepdims=True)
    acc = alpha * acc + _dot(p.astype(BF16), v)
    return m_new, l, acc


def _nsa_attn_kernel(qa_ref, ks_ref, vs_ref, kw_ref, vw_ref, ocmp_ref, gate_ref, place_ref, o_ref):
    tq = qa_ref.shape[0]
    tk = ATT_TK
    i = pl.program_id(2)
    hd = NSA_HEAD_DIM
    r_io = lax.broadcasted_iota(jnp.int32, (tq, tk), 0)
    c_io = lax.broadcasted_iota(jnp.int32, (tq, tk), 1)
    causal = c_io <= r_io
    gates = _sigmoid(gate_ref[...])
    init = (jnp.full((tq, 1), NEG_INF, F32), jnp.zeros((tq, 1), F32), jnp.zeros((tq, hd), F32))
    k_diag_s = ks_ref[0, 0, pl.ds(pl.multiple_of(i * tk, tk), tk), :]
    v_diag_s = vs_ref[0, 0, pl.ds(pl.multiple_of(i * tk, tk), tk), :]
    k_diag_w = kw_ref[0, 0, pl.ds(pl.multiple_of(i * tk, tk), tk), :]
    v_diag_w = vw_ref[0, 0, pl.ds(pl.multiple_of(i * tk, tk), tk), :]
    n_win_back = WINDOW // tk

    out = jnp.zeros((tq, NSA_HPG * hd), F32)
    for h in range(NSA_HPG):
        q = qa_ref[:, h * LANES:(h + 1) * LANES]

        def sel_body(j, carry):
            off = pl.multiple_of(j * tk, tk)
            return _flash_step(q, ks_ref[0, 0, pl.ds(off, tk), :], vs_ref[0, 0, pl.ds(off, tk), :],
                               None, carry)

        m, l, acc = lax.fori_loop(0, i, sel_body, init)
        m, l, acc = _flash_step(q, k_diag_s, v_diag_s, causal, (m, l, acc))
        o_sel = acc / l

        def win_body(j, carry):
            off = pl.multiple_of(j * tk, tk)
            vis = c_io + (j - i) * tk > r_io - WINDOW
            return _flash_step(q, kw_ref[0, 0, pl.ds(off, tk), :], vw_ref[0, 0, pl.ds(off, tk), :],
                               vis, carry)

        m, l, acc = lax.fori_loop(jnp.maximum(i - n_win_back, 0), i, win_body, init)
        m, l, acc = _flash_step(q, k_diag_w, v_diag_w, causal, (m, l, acc))
        o_win = acc / l

        gb = h * N_BRANCHES
        o_h = (gates[:, gb:gb + 1] * ocmp_ref[0, 0, h]
               + gates[:, gb + 1:gb + 2] * o_sel
               + gates[:, gb + 2:gb + 3] * o_win)
        out = out + _dot(o_h.astype(BF16), place_ref[h])
    o_ref[...] = out.astype(BF16)


def _nsa_attn(qa, ks, vs, kw, vw, ocmp, gates, place, batch, seq_len):
    t = qa.shape[0]
    tq = ATT_TQ
    nq = seq_len // tq
    g, hpg, hd = NSA_KV_GROUPS, NSA_HPG, NSA_HEAD_DIM
    gw = hpg * LANES
    kspec = pl.BlockSpec((1, 1, seq_len, LANES), lambda b, gi, i: (b, gi, 0, 0))
    vspec = pl.BlockSpec((1, 1, seq_len, hd), lambda b, gi, i: (b, gi, 0, 0))
    return pl.pallas_call(
        _nsa_attn_kernel,
        grid=(batch, g, nq),
        in_specs=[pl.BlockSpec((tq, gw), lambda b, gi, i: (b * nq + i, gi)),
                  kspec, vspec, kspec, vspec,
                  pl.BlockSpec((1, 1, hpg, tq, hd), lambda b, gi, i: (b, gi, 0, i, 0)),
                  pl.BlockSpec((tq, LANES), lambda b, gi, i: (b * nq + i, gi)),
                  pl.BlockSpec(place.shape, lambda b, gi, i: (0, 0, 0))],
        out_specs=pl.BlockSpec((tq, hpg * hd), lambda b, gi, i: (b * nq + i, gi)),
        out_shape=jax.ShapeDtypeStruct((t, g * hpg * hd), BF16),
        compiler_params=_cparams("parallel", "parallel", "arbitrary"),
        name="nsa_attention",
    )(qa, ks, vs, kw, vw, ocmp, gates, place)


def _nsa_constants(seq_len):
    nsb = seq_len // SEL_BLOCK
    nc = CMP_ROWS
    cs = np.arange(nc) * COMP_STRIDE
    ss = np.arange(nsb) * SEL_BLOCK
    ov = (np.minimum(cs[:, None] + COMP_BLOCK, ss[None, :] + SEL_BLOCK)
          - np.maximum(cs[:, None], ss[None, :]))
    ov = np.clip(ov, 0, None).astype(np.float32) / COMP_BLOCK
    ov[(seq_len - COMP_BLOCK) // COMP_STRIDE + 1:, :] = 0.0
    ov_t = jnp.asarray(ov.T, BF16)
    place_q = np.zeros((LANES, LANES), np.float32)
    place_q[np.arange(nsb), NSA_HEAD_DIM + np.arange(nsb)] = 1.0
    place_o = np.zeros((NSA_HPG, NSA_HEAD_DIM, NSA_HPG * NSA_HEAD_DIM), np.float32)
    for h in range(NSA_HPG):
        place_o[h, np.arange(NSA_HEAD_DIM), h * NSA_HEAD_DIM + np.arange(NSA_HEAD_DIM)] = 1.0
    onehot = np.zeros((seq_len, LANES - NSA_HEAD_DIM), np.float32)
    onehot[np.arange(seq_len), np.arange(seq_len) // SEL_BLOCK] = 1.0
    return ov_t, jnp.asarray(place_q, BF16), jnp.asarray(place_o, BF16), jnp.asarray(onehot, BF16)


def _prep_nsa_weights(w_in, cmp_pos, cmp_w1, cmp_w2, w_out):
    d, g, hpg, hd = D_MODEL, NSA_KV_GROUPS, NSA_HPG, NSA_HEAD_DIM
    kv_end = NSA_Q_DIM + 2 * N_BRANCHES * NSA_KV_DIM
    scale = hd ** -0.5
    wq = (w_in[:, :NSA_Q_DIM] * scale).reshape(d, NSA_HEADS, hd)
    wq = jnp.pad(wq, ((0, 0), (0, 0), (0, LANES - hd))).reshape(d, NSA_HEADS * LANES)
    w_kvc = w_in[:, NSA_Q_DIM:NSA_Q_DIM + 2 * NSA_KV_DIM]
    w_kvsw = w_in[:, NSA_Q_DIM + 2 * NSA_KV_DIM:kv_end]
    wg = w_in[:, kv_end:].reshape(d, g, hpg * N_BRANCHES)
    wg = jnp.pad(wg, ((0, 0), (0, 0), (0, LANES - hpg * N_BRANCHES))).reshape(d, g * LANES)
    w_all = jnp.concatenate([wq, w_kvc, w_kvsw, wg], axis=1).astype(BF16)
    half = (COMP_BLOCK // 2) * hd
    pos = cmp_pos.reshape(2, 2, 1, half)
    w1a = cmp_w1[:, :half].astype(BF16)
    w1b = cmp_w1[:, half:].astype(BF16)
    return w_all, pos, w1a, w1b, cmp_w2.astype(BF16), w_out.astype(BF16)


def _nsa_layer(h, g_pre, g_post, weights, consts, batch, seq_len):
    w_all, pos, w1a, w1b, w2, w_out = weights
    ov_t, place_q, place_o, onehot = consts
    g, hd = NSA_KV_GROUPS, NSA_HEAD_DIM
    qpad, kvc, kvsw, gates = _norm_proj(
        h, g_pre, w_all,
        [(NSA_HEADS * LANES, BF16), (2 * NSA_KV_DIM, F32), (4 * NSA_KV_DIM, BF16), (g * LANES, F32)])

    per_row = COMP_BLOCK // 2
    a = kvc.reshape(batch, seq_len // per_row, per_row, 2, g, hd).transpose(3, 0, 4, 1, 2, 5)
    a = a.reshape(2, batch * g * CMP_ROWS, per_row * hd)
    kcvc = _compress(a, pos, w1a, w1b, w2)
    kcp = jnp.pad(kcvc[0], ((0, 0), (0, LANES - hd))).astype(BF16).reshape(batch * g, CMP_ROWS, LANES)
    vc = kcvc[1].astype(BF16).reshape(batch * g, CMP_ROWS, hd)
    ocmp, qa = _cmp_sel(qpad, kcp, vc, ov_t, place_q, batch, seq_len)

    kvr = kvsw.reshape(batch, seq_len, 4, g, hd).transpose(2, 0, 3, 1, 4)
    oh = jnp.broadcast_to(onehot, (batch, g, seq_len, LANES - hd))
    ks = jnp.concatenate([kvr[0], oh], axis=-1)
    kw = jnp.pad(kvr[2], ((0, 0), (0, 0), (0, 0), (0, LANES - hd)))
    o = _nsa_attn(qa, ks, kvr[1], kw, kvr[3], ocmp, gates, place_o, batch, seq_len)
    return _out_proj(o, w_out, g_post, h)


def _ssd_in_kernel(x_ref, g_ref, w_ref, cw_ref, dtb_ref, z_ref, xbc_ref, dt_ref, xe_ref, *, tiles_per_seq):
    tm = x_ref.shape[0]
    xn = _rms(x_ref[...], g_ref[...]).astype(BF16)
    _fill_halo(xe_ref, xn, tm, pl.program_id(0) % tiles_per_seq == 0)
    z_ref[...] = _dot(xn, w_ref[:, :SSD_D_INNER]).astype(z_ref.dtype)
    c0, c1 = SSD_D_INNER, SSD_D_INNER + SSD_CONV_DIM
    u = _causal_conv(_dot(xe_ref[...], w_ref[:, c0:c1]), cw_ref[...], SSD_CONV)
    xbc_ref[...] = (u * _sigmoid(u)).astype(xbc_ref.dtype)
    dt_raw = _dot(xn, w_ref[:, c1:]) + dtb_ref[...]
    dt_ref[...] = jnp.maximum(dt_raw, 0.0) + jnp.log1p(jnp.exp(-jnp.abs(dt_raw)))


def _ssd_in(x, gain, w, cw, dtb, seq_len):
    t, d = x.shape
    tm = PROJ_TM
    kern = functools.partial(_ssd_in_kernel, tiles_per_seq=seq_len // tm)
    return pl.pallas_call(
        kern,
        grid=(t // tm,),
        in_specs=[pl.BlockSpec((tm, d), lambda i: (i, 0)),
                  _resident((1, d)), _resident(w.shape), _resident(cw.shape), _resident(dtb.shape)],
        out_specs=[pl.BlockSpec((tm, SSD_D_INNER), lambda i: (i, 0)),
                   pl.BlockSpec((tm, SSD_CONV_DIM), lambda i: (i, 0)),
                   pl.BlockSpec((tm, LANES), lambda i: (i, 0))],
        out_shape=[jax.ShapeDtypeStruct((t, SSD_D_INNER), BF16),
                   jax.ShapeDtypeStruct((t, SSD_CONV_DIM), BF16),
                   jax.ShapeDtypeStruct((t, LANES), F32)],
        scratch_shapes=[pltpu.VMEM((tm + HALO, d), BF16)],
        compiler_params=_cparams("arbitrary"),
        name="ssd_in_proj",
    )(x, gain, w, cw, dtb)


def _ssd_scan_kernel(xbc_ref, z_ref, dt_ref, alog_ref, dskip_ref, nw_ref, tri_ref, y_ref, state_ref):
    L = SSD_CHUNK
    hd = SSD_HEAD_DIM
    gw = SSD_GROUP_WIDTH
    n = SSD_STATE

    @pl.when(pl.program_id(1) == 0)
    def _():
        state_ref[...] = jnp.zeros(state_ref.shape, F32)

    dt = dt_ref[...]
    dta = dt * (-jnp.exp(alog_ref[...]))
    tri = tri_ref[...]
    acs = sum(_dot(tri, part) for part in _split_bf16(dta, 3))
    acs_t = acs.T
    dt_t = dt.T
    last = acs[L - 1:L, :]
    e_in = jnp.exp(acs)
    dec = jnp.exp(last - acs) * dt
    e_last = jnp.exp(last)
    causal = (lax.broadcasted_iota(jnp.int32, (L, L), 0) >= lax.broadcasted_iota(jnp.int32, (L, L), 1))
    lo_half = lax.broadcasted_iota(jnp.int32, (L, LANES), 1) < hd
    lo_row = lo_half[0:1, :]
    zero_bf = jnp.zeros((L, LANES), BF16)

    for g in range(SSD_GROUPS):
        b_g = xbc_ref[:, SSD_D_INNER + g * n:SSD_D_INNER + (g + 1) * n]
        c_g = xbc_ref[:, SSD_D_INNER + (SSD_GROUPS + g) * n:SSD_D_INNER + (SSD_GROUPS + g + 1) * n]
        cb = _dot_nt(c_g, b_g)
        b_t = b_g.astype(F32).T.astype(BF16)
        st = state_ref[g]
        y_inter = _dot(c_g, st.astype(BF16))
        x_g = xbc_ref[:, g * gw:(g + 1) * gw]
        ys, xds, els = [], [], []
        for pr in range(SSD_HEADS_PER_GROUP // 2):
            heads = (g * SSD_HEADS_PER_GROUP + 2 * pr, g * SSD_HEADS_PER_GROUP + 2 * pr + 1)
            xp = x_g[:, pr * LANES:(pr + 1) * LANES]
            ws = []
            for h in heads:
                seg = acs[:, h:h + 1] - acs_t[h:h + 1, :]
                lm = jnp.exp(jnp.where(causal, seg, NEG_INF))
                ws.append((cb * lm * dt_t[h:h + 1, :]).astype(BF16))
            w2 = jnp.concatenate(ws, axis=1)
            x2 = jnp.concatenate([jnp.where(lo_half, xp, zero_bf), jnp.where(lo_half, zero_bf, xp)], axis=0)
            h0, h1 = heads
            e_p = jnp.where(lo_half, e_in[:, h0:h0 + 1], e_in[:, h1:h1 + 1])
            d_p = jnp.where(lo_half, dec[:, h0:h0 + 1], dec[:, h1:h1 + 1])
            ys.append(_dot(w2, x2) + y_inter[:, pr * LANES:(pr + 1) * LANES] * e_p)
            xds.append((xp.astype(F32) * d_p).astype(BF16))
            els.append(jnp.where(lo_row, e_last[:, h0:h0 + 1], e_last[:, h1:h1 + 1]))
        xd = jnp.concatenate(xds, axis=1)
        state_ref[g] = st * jnp.concatenate(els, axis=1) + _dot(b_t, xd)
        sl = slice(g * gw, (g + 1) * gw)
        yg = jnp.concatenate(ys, axis=1) + dskip_ref[:, sl] * x_g.astype(F32)
        zg = z_ref[:, sl].astype(F32)
        yg = yg * (zg * _sigmoid(zg))
        y_ref[:, sl] = _rms(yg, nw_ref[:, sl]).astype(y_ref.dtype)


def _ssd_scan(xbc, z, dt, alog, dskip, nw, tri, batch, seq_len):
    t = xbc.shape[0]
    L = SSD_CHUNK
    nc = seq_len // L
    row = lambda b, c: (b * nc + c, 0)
    return pl.pallas_call(
        _ssd_scan_kernel,
        grid=(batch, nc),
        in_specs=[pl.BlockSpec((L, SSD_CONV_DIM), row),
                  pl.BlockSpec((L, SSD_D_INNER), row),
                  pl.BlockSpec((L, LANES), row),
                  pl.BlockSpec((1, LANES), lambda b, c: (0, 0)),
                  pl.BlockSpec((1, SSD_D_INNER), lambda b, c: (0, 0)),
                  pl.BlockSpec((1, SSD_D_INNER), lambda b, c: (0, 0)),
                  pl.BlockSpec((L, L), lambda b, c: (0, 0))],
        out_specs=pl.BlockSpec((L, SSD_D_INNER), row),
        out_shape=jax.ShapeDtypeStruct((t, SSD_D_INNER), BF16),
        scratch_shapes=[pltpu.VMEM((SSD_GROUPS, SSD_STATE, SSD_GROUP_WIDTH), F32)],
        compiler_params=_cparams("parallel", "arbitrary"),
        name="ssd_scan",
    )(xbc, z, dt, alog, dskip, nw, tri)


def _prep_ssd_weights(w_in, conv_w, conv_b, dt_bias, a_log, d_skip, norm_w, w_out):
    pad = LANES - SSD_HEADS
    w_all = jnp.pad(w_in, ((0, 0), (0, pad))).astype(BF16)
    cw = jnp.concatenate([conv_w, conv_b[None, :], jnp.zeros((8 - SSD_CONV - 1, SSD_CONV_DIM), F32)], axis=0)
    dtb = jnp.pad(dt_bias, (0, pad))[None, :]
    alog = jnp.pad(a_log, (0, pad))[None, :]
    dskip = jnp.repeat(d_skip, SSD_HEAD_DIM)[None, :]
    return w_all, cw, dtb, alog, dskip, norm_w[None, :], w_out.astype(BF16)


def _ssd_layer(h, g_pre, g_post, weights, tri, batch, seq_len):
    w_all, cw, dtb, alog, dskip, nw, w_out = weights
    z, xbc, dt = _ssd_in(h, g_pre, w_all, cw, dtb, seq_len)
    y = _ssd_scan(xbc, z, dt, alog, dskip, nw, tri, batch, seq_len)
    return _out_proj(y, w_out, g_post, h)


def kernel(x, norm_gains, nsa_w_in, nsa_cmp_pos, nsa_cmp_w1, nsa_cmp_w2, nsa_w_out, ssd_w_in, ssd_conv_w, ssd_conv_b, ssd_dt_bias, ssd_a_log, ssd_d, ssd_norm_w, ssd_w_out, ffn_w_up, ffn_conv_w, ffn_conv_b, ffn_w_down):
    batch, seq_len, d = x.shape
    assert d == D_MODEL and seq_len % FFN_TM == 0 and seq_len % ATT_TQ == 0 and seq_len % SSD_CHUNK == 0
    assert seq_len // (COMP_BLOCK // 2) == CMP_ROWS and (batch * NSA_KV_GROUPS) % CMP_SEQ_PER_STEP == 0
    assert ATT_TQ == ATT_TK and WINDOW % ATT_TK == 0 and seq_len // SEL_BLOCK <= LANES - NSA_HEAD_DIM
    h = x.reshape(batch * seq_len, d)
    nsa_consts = _nsa_constants(seq_len)
    tri = jnp.asarray(np.tril(np.ones((SSD_CHUNK, SSD_CHUNK), np.float32)), BF16)
    for i in range(DEPTH):
        gains = norm_gains[i][:, None, :]
        slot = i // N_MIXERS
        if i % N_MIXERS == 0:
            w = _prep_nsa_weights(nsa_w_in[slot], nsa_cmp_pos[slot], nsa_cmp_w1[slot],
                                  nsa_cmp_w2[slot], nsa_w_out[slot])
            h = _nsa_layer(h, gains[0], gains[1], w, nsa_consts, batch, seq_len)
        else:
            w = _prep_ssd_weights(ssd_w_in[slot], ssd_conv_w[slot], ssd_conv_b[slot], ssd_dt_bias[slot],
                                  ssd_a_log[slot], ssd_d[slot], ssd_norm_w[slot], ssd_w_out[slot])
            h = _ssd_layer(h, gains[0], gains[1], w, tri, batch, seq_len)
        fw = _prep_ffn_weights(ffn_w_up[i], ffn_conv_w[i], ffn_conv_b[i], ffn_w_down[i])
        h = _ffn(h, gains[2], *fw, gains[3], seq_len)
    return h.reshape(batch, seq_len, d)
```

```python
import functools

import numpy as np
import jax
import jax.numpy as jnp
from jax import lax
from jax.experimental import pallas as pl
from jax.experimental.pallas import tpu as pltpu

F32 = jnp.float32
BF16 = jnp.bfloat16

D_MODEL = 1024
DEPTH = 4
N_MIXERS = 2

NSA_HEADS = 16
NSA_HEAD_DIM = 64
NSA_KV_GROUPS = 4
NSA_HPG = NSA_HEADS // NSA_KV_GROUPS
COMP_BLOCK = 32
COMP_STRIDE = 16
COMP_HIDDEN = 256
SEL_BLOCK = 64
N_SELECT = 16
N_LOCAL_BLOCKS = 2
WINDOW = 512
N_BRANCHES = 3
NSA_Q_DIM = NSA_HEADS * NSA_HEAD_DIM
NSA_KV_DIM = NSA_KV_GROUPS * NSA_HEAD_DIM

SSD_D_INNER = 2 * D_MODEL
SSD_HEAD_DIM = 64
SSD_HEADS = SSD_D_INNER // SSD_HEAD_DIM
SSD_GROUPS = 4
SSD_STATE = 128
SSD_CONV = 4
SSD_CHUNK = 128
SSD_CONV_DIM = SSD_D_INNER + 2 * SSD_GROUPS * SSD_STATE
SSD_HEADS_PER_GROUP = SSD_HEADS // SSD_GROUPS
SSD_GROUP_WIDTH = SSD_D_INNER // SSD_GROUPS

FFN_HIDDEN = 2816
FFN_CONV = 3

RMS_EPS = 1e-6
NEG_INF = -1e30
SEL_FORCE = 1e9
MASK_BIG = 2.0 ** 100

LANES = 128
HALO = 16
VMEM_LIMIT = 56 * 1024 * 1024

PROJ_TM = 256
OUT_TM = 512
FFN_TM = 512
FFN_FC = 256
ATT_TQ = 256
ATT_TK = 256
CMP_ROWS = 128
CMP_SEQ_PER_STEP = 8


def _cparams(*sem):
    return pltpu.CompilerParams(dimension_semantics=sem, vmem_limit_bytes=VMEM_LIMIT)


def _resident(shape):
    nd = len(shape)
    return pl.BlockSpec(shape, lambda *_: (0,) * nd, pipeline_mode=pl.Buffered(1))


def _rms(x, gain):
    ms = jnp.mean(x * x, axis=-1, keepdims=True)
    return x * lax.rsqrt(ms + RMS_EPS) * gain


def _sigmoid(x):
    return 1.0 / (1.0 + jnp.exp(-x))


def _dot(a, b):
    return jnp.dot(a, b, preferred_element_type=F32)


def _dot_nt(a, b):
    return lax.dot_general(a, b, (((1,), (1,)), ((), ())), preferred_element_type=F32)


def _split_bf16(x, parts):
    out = []
    r = x
    for _ in range(parts):
        t = r.astype(BF16)
        out.append(t)
        r = r - t.astype(F32)
    return out


def _causal_conv(u_ext, cw, width):
    y = cw[width - 1:width, :] * u_ext[HALO:, :] + cw[width:width + 1, :]
    for back in range(1, width):
        shifted = pltpu.roll(u_ext, back, axis=0)[HALO:, :]
        y = y + cw[width - 1 - back:width - back, :] * shifted
    return y


def _fill_halo(xe_ref, xn, tm, first_of_seq):
    @pl.when(first_of_seq)
    def _():
        xe_ref[0:HALO, :] = jnp.zeros((HALO, xe_ref.shape[1]), xe_ref.dtype)

    @pl.when(jnp.logical_not(first_of_seq))
    def _():
        xe_ref[0:HALO, :] = xe_ref[tm:tm + HALO, :]

    xe_ref[HALO:, :] = xn


def _norm_proj_kernel(x_ref, g_ref, w_ref, *out_refs):
    xn = _rms(x_ref[...], g_ref[...]).astype(BF16)
    off = 0
    for o_ref in out_refs:
        n = o_ref.shape[1]
        o_ref[...] = _dot(xn, w_ref[:, off:off + n]).astype(o_ref.dtype)
        off += n


def _norm_proj(x, gain, w, outs):
    t, d = x.shape
    tm = PROJ_TM
    return pl.pallas_call(
        _norm_proj_kernel,
        grid=(t // tm,),
        in_specs=[pl.BlockSpec((tm, d), lambda i: (i, 0)),
                  _resident((1, d)),
                  _resident(w.shape)],
        out_specs=[pl.BlockSpec((tm, n), lambda i: (i, 0)) for n, _ in outs],
        out_shape=[jax.ShapeDtypeStruct((t, n), dt) for n, dt in outs],
        compiler_params=_cparams("parallel"),
        name="norm_proj",
    )(x, gain, w)


def _out_proj_kernel(y_ref, w_ref, g_ref, h_ref, o_ref):
    f = _dot(y_ref[...], w_ref[...])
    o_ref[...] = h_ref[...] + _rms(f, g_ref[...])


def _out_proj(y, w, gain, h):
    t, k = y.shape
    d = w.shape[1]
    tm = OUT_TM
    return pl.pallas_call(
        _out_proj_kernel,
        grid=(t // tm,),
        in_specs=[pl.BlockSpec((tm, k), lambda i: (i, 0)),
                  _resident(w.shape),
                  _resident((1, d)),
                  pl.BlockSpec((tm, d), lambda i: (i, 0))],
        out_specs=pl.BlockSpec((tm, d), lambda i: (i, 0)),
        out_shape=jax.ShapeDtypeStruct((t, d), F32),
        compiler_params=_cparams("parallel"),
        name="out_proj",
    )(y, w, gain, h)


def _ffn_kernel(x_ref, g1_ref, wv_ref, wg_ref, cv_ref, cg_ref, wd_ref, g2_ref, o_ref, xe_ref,
                *, tiles_per_seq):
    tm = x_ref.shape[0]
    x = x_ref[...]
    xn = _rms(x, g1_ref[...]).astype(BF16)
    _fill_halo(xe_ref, xn, tm, pl.program_id(0) % tiles_per_seq == 0)
    xe = xe_ref[...]

    def chunk(c, acc):
        val = _causal_conv(_dot(xe, wv_ref[c]), cv_ref[c], FFN_CONV)
        gate = _causal_conv(_dot(xe, wg_ref[c]), cg_ref[c], FFN_CONV)
        act = (gate * _sigmoid(gate) * val).astype(BF16)
        return acc + _dot(act, wd_ref[c])

    f = lax.fori_loop(0, wv_ref.shape[0], chunk, jnp.zeros((tm, D_MODEL), F32))
    o_ref[...] = x + _rms(f, g2_ref[...])


def _ffn(x, g1, wv, wg, cv, cg, wd, g2, seq_len):
    t, d = x.shape
    tm = FFN_TM
    kern = functools.partial(_ffn_kernel, tiles_per_seq=seq_len // tm)
    return pl.pallas_call(
        kern,
        grid=(t // tm,),
        in_specs=[pl.BlockSpec((tm, d), lambda i: (i, 0)),
                  _resident((1, d)),
                  _resident(wv.shape), _resident(wg.shape),
                  _resident(cv.shape), _resident(cg.shape),
                  _resident(wd.shape),
                  _resident((1, d))],
        out_specs=pl.BlockSpec((tm, d), lambda i: (i, 0)),
        out_shape=jax.ShapeDtypeStruct((t, d), F32),
        scratch_shapes=[pltpu.VMEM((tm + HALO, d), BF16)],
        compiler_params=_cparams("arbitrary"),
        name="conv_ffn",
    )(x, g1, wv, wg, cv, cg, wd, g2)


def _prep_ffn_weights(w_up, conv_w, conv_b, w_down):
    f, fc = FFN_HIDDEN, FFN_FC
    nch = f // fc

    def chunk_cols(w):
        return w.reshape(w.shape[0], nch, fc).transpose(1, 0, 2)

    def conv_tab(cw, cb):
        tab = jnp.concatenate([cw, cb[None, :], jnp.zeros((8 - FFN_CONV - 1, f), F32)], axis=0)
        return chunk_cols(tab)

    wv = chunk_cols(w_up[:, :f]).astype(BF16)
    wg = chunk_cols(w_up[:, f:]).astype(BF16)
    cv = conv_tab(conv_w[:, :f], conv_b[:f])
    cg = conv_tab(conv_w[:, f:], conv_b[f:])
    wd = w_down.reshape(nch, fc, D_MODEL).astype(BF16)
    return wv, wg, cv, cg, wd


def _gelu_tanh(x):
    return 0.5 * x * (1.0 + jnp.tanh(float(np.sqrt(2.0 / np.pi)) * (x + 0.044715 * (x * x * x))))


def _compress_kernel(a_ref, pos_ref, w1a_ref, w1b_ref, w2_ref, o_ref):
    a = a_ref[0]
    m = a.shape[0]
    u1 = _dot((a + pos_ref[0, 0]).astype(BF16), w1a_ref[0])
    u2 = _dot((a + pos_ref[0, 1]).astype(BF16), w1b_ref[0])
    hid = u1 + pltpu.roll(u2, m - 1, axis=0)
    o_ref[0] = _dot(_gelu_tanh(hid).astype(BF16), w2_ref[0])


def _compress(a, pos, w1a, w1b, w2):
    _, r, k = a.shape
    m = CMP_SEQ_PER_STEP * CMP_ROWS
    return pl.pallas_call(
        _compress_kernel,
        grid=(2, r // m),
        in_specs=[pl.BlockSpec((1, m, k), lambda kv, i: (kv, i, 0)),
                  pl.BlockSpec((1, 2, 1, k), lambda kv, i: (kv, 0, 0, 0)),
                  pl.BlockSpec((1, k, COMP_HIDDEN), lambda kv, i: (kv, 0, 0)),
                  pl.BlockSpec((1, k, COMP_HIDDEN), lambda kv, i: (kv, 0, 0)),
                  pl.BlockSpec((1, COMP_HIDDEN, NSA_HEAD_DIM), lambda kv, i: (kv, 0, 0))],
        out_specs=pl.BlockSpec((1, m, NSA_HEAD_DIM), lambda kv, i: (kv, i, 0)),
        out_shape=jax.ShapeDtypeStruct((2, r, NSA_HEAD_DIM), F32),
        compiler_params=_cparams("parallel", "parallel"),
        name="nsa_compress",
    )(a, pos, w1a, w1b, w2)


def _cmp_sel_kernel(q_ref, kc_ref, vc_ref, ov_ref, place_ref, ocmp_ref, qa_ref):
    tq = q_ref.shape[0]
    q0 = pl.program_id(2) * tq
    kc = kc_ref[0]
    vc = vc_ref[0]
    nsb = ov_ref.shape[0]

    c_io = lax.broadcasted_iota(jnp.int32, (CMP_ROWS, tq), 0)
    pos_c = q0 + lax.broadcasted_iota(jnp.int32, (CMP_ROWS, tq), 1)
    cmask = c_io * COMP_STRIDE + (COMP_BLOCK - 1) <= pos_c

    psum = jnp.zeros((CMP_ROWS, tq), F32)
    for h in range(NSA_HPG):
        qh = q_ref[:, h * LANES:(h + 1) * LANES]
        st = jnp.where(cmask, _dot_nt(kc, qh), NEG_INF)
        mx = jnp.max(st, axis=0, keepdims=True)
        e = jnp.where(cmask, jnp.exp2(st - mx), 0.0)
        den = jnp.sum(e, axis=0, keepdims=True)
        p = e / jnp.where(den > 0.0, den, 1.0)
        psum = psum + p
        ocmp_ref[0, 0, h] = _dot(p.T.astype(BF16), vc)

    hi, lo = _split_bf16(psum, 2)
    pslc = _dot(ov_ref[...], hi) + _dot(ov_ref[...], lo)

    j_io = lax.broadcasted_iota(jnp.int32, (nsb, tq), 0)
    pos_j = q0 + lax.broadcasted_iota(jnp.int32, (nsb, tq), 1)
    cur = jnp.right_shift(pos_j, SEL_BLOCK.bit_length() - 1)
    allowed = j_io * SEL_BLOCK <= pos_j
    forced = (j_io == 0) | ((j_io <= cur) & (j_io > cur - N_LOCAL_BLOCKS))
    score = jnp.where(forced, SEL_FORCE, jnp.where(allowed, pslc, NEG_INF))

    rank = jnp.zeros((nsb, tq), F32)
    for i in range(nsb):
        row = score[i:i + 1, :]
        before = (row > score) | ((row == score) & (j_io > i))
        rank = rank + jnp.where(before, 1.0, 0.0)
    bias = jnp.where(rank < float(min(N_SELECT, nsb)), 0.0, -MASK_BIG)
    bias = jnp.concatenate([bias, jnp.zeros((LANES - nsb, tq), F32)], axis=0)
    placed = _dot(bias.T.astype(BF16), place_ref[...])
    for h in range(NSA_HPG):
        qa_ref[0, 0, h] = (q_ref[:, h * LANES:(h + 1) * LANES].astype(F32) + placed).astype(BF16)


def _cmp_sel(qpad, kcp, vc, ov_t, place, batch, seq_len):
    t = qpad.shape[0]
    tq = ATT_TQ
    nq = seq_len // tq
    g, hpg = NSA_KV_GROUPS, NSA_HPG
    gw = hpg * LANES
    return pl.pallas_call(
        _cmp_sel_kernel,
        grid=(batch, g, nq),
        in_specs=[pl.BlockSpec((tq, gw), lambda b, gi, i: (b * nq + i, gi)),
                  pl.BlockSpec((1, CMP_ROWS, LANES), lambda b, gi, i: (b * g + gi, 0, 0)),
                  pl.BlockSpec((1, CMP_ROWS, NSA_HEAD_DIM), lambda b, gi, i: (b * g + gi, 0, 0)),
                  pl.BlockSpec(ov_t.shape, lambda b, gi, i: (0, 0)),
                  pl.BlockSpec(place.shape, lambda b, gi, i: (0, 0))],
        out_specs=[pl.BlockSpec((1, 1, hpg, tq, NSA_HEAD_DIM), lambda b, gi, i: (b, gi, 0, i, 0)),
                   pl.BlockSpec((1, 1, hpg, tq, LANES), lambda b, gi, i: (b, gi, 0, i, 0))],
        out_shape=[jax.ShapeDtypeStruct((batch, g, hpg, seq_len, NSA_HEAD_DIM), F32),
                   jax.ShapeDtypeStruct((batch, g, hpg, seq_len, LANES), BF16)],
        compiler_params=_cparams("parallel", "parallel", "parallel"),
        name="nsa_cmp_select",
    )(qpad, kcp, vc, ov_t, place)


def _softmax_pv(s, v, m, acc):
    m_new = jnp.maximum(m, jnp.max(s, axis=-1, keepdims=True))
    p = jnp.exp2(s - m_new).astype(BF16)
    return m_new, jnp.exp2(m - m_new) * acc + _dot(p, v)


def _nsa_attn_kernel(qa_ref, ks_ref, vs_ref, kw_ref, vw_ref, ocmp_ref, gate_ref, place_ref, o_ref):
    hpg, hd = NSA_HPG, NSA_HEAD_DIM
    tq = qa_ref.shape[3]
    tk = ATT_TK
    i = pl.program_id(2)
    rows = hpg * tq
    q = qa_ref[0, 0].reshape(rows, LANES)

    def tile(ref, j, n):
        return ref[0, 0, pl.ds(pl.multiple_of(j * tk, tk), n), :]

    def sel_body(j, carry):
        s, m, acc = carry
        s_next = _dot_nt(q, tile(ks_ref, j + 1, tk))
        m, acc = _softmax_pv(s, tile(vs_ref, j, tk), m, acc)
        return s_next, m, acc

    init = (_dot_nt(q, tile(ks_ref, 0, tk)), jnp.full((rows, 1), NEG_INF, F32), jnp.zeros((rows, LANES), F32))
    s, m, acc = lax.fori_loop(0, i, sel_body, init)
    r_io = lax.broadcasted_iota(jnp.int32, (rows, tk), 0) & (tq - 1)
    c_io = lax.broadcasted_iota(jnp.int32, (rows, tk), 1)
    s = jnp.where(c_io <= r_io, s, NEG_INF)
    _, acc = _softmax_pv(s, tile(vs_ref, i, tk), m, acc)
    o_sel = acc[:, :hd] / acc[:, hd:hd + 1]

    nwk = WINDOW + tk
    j0 = jnp.maximum(i - WINDOW // tk, 0)
    s = _dot_nt(q, tile(kw_ref, j0, nwk))
    qpos = i * tq + (lax.broadcasted_iota(jnp.int32, (rows, nwk), 0) & (tq - 1))
    kpos = j0 * tk + lax.broadcasted_iota(jnp.int32, (rows, nwk), 1)
    s = jnp.where((kpos <= qpos) & (kpos > qpos - WINDOW), s, NEG_INF)
    p = jnp.exp2(s - jnp.max(s, axis=-1, keepdims=True)).astype(BF16)
    acc = _dot(p, tile(vw_ref, j0, nwk))
    o_win = acc[:, :hd] / acc[:, hd:hd + 1]

    gates = _sigmoid(gate_ref[...])
    out = jnp.zeros((tq, hpg * hd), F32)
    for h in range(hpg):
        gb = h * N_BRANCHES
        rs = slice(h * tq, (h + 1) * tq)
        o_h = (gates[:, gb:gb + 1] * ocmp_ref[0, 0, h]
               + gates[:, gb + 1:gb + 2] * o_sel[rs]
               + gates[:, gb + 2:gb + 3] * o_win[rs])
        out = out + _dot(o_h.astype(BF16), place_ref[h])
    o_ref[...] = out.astype(BF16)


def _nsa_attn(qa, ks, vs, kw, vw, ocmp, gates, place, batch, seq_len):
    t = batch * seq_len
    tq = ATT_TQ
    nq = seq_len // tq
    g, hpg, hd = NSA_KV_GROUPS, NSA_HPG, NSA_HEAD_DIM
    kvspec = pl.BlockSpec((1, 1, seq_len, LANES), lambda b, gi, i: (b, gi, 0, 0))
    return pl.pallas_call(
        _nsa_attn_kernel,
        grid=(batch, g, nq),
        in_specs=[pl.BlockSpec((1, 1, hpg, tq, LANES), lambda b, gi, i: (b, gi, 0, i, 0)),
                  kvspec, kvspec, kvspec, kvspec,
                  pl.BlockSpec((1, 1, hpg, tq, hd), lambda b, gi, i: (b, gi, 0, i, 0)),
                  pl.BlockSpec((tq, LANES), lambda b, gi, i: (b * nq + i, gi)),
                  pl.BlockSpec(place.shape, lambda b, gi, i: (0, 0, 0))],
        out_specs=pl.BlockSpec((tq, hpg * hd), lambda b, gi, i: (b * nq + i, gi)),
        out_shape=jax.ShapeDtypeStruct((t, g * hpg * hd), BF16),
        compiler_params=_cparams("parallel", "parallel", "arbitrary"),
        name="nsa_attention",
    )(qa, ks, vs, kw, vw, ocmp, gates, place)


def _nsa_constants(seq_len):
    nsb = seq_len // SEL_BLOCK
    nc = CMP_ROWS
    cs = np.arange(nc) * COMP_STRIDE
    ss = np.arange(nsb) * SEL_BLOCK
    ov = (np.minimum(cs[:, None] + COMP_BLOCK, ss[None, :] + SEL_BLOCK)
          - np.maximum(cs[:, None], ss[None, :]))
    ov = np.clip(ov, 0, None).astype(np.float32) / COMP_BLOCK
    ov[(seq_len - COMP_BLOCK) // COMP_STRIDE + 1:, :] = 0.0
    ov_t = jnp.asarray(ov.T, BF16)
    place_q = np.zeros((LANES, LANES), np.float32)
    place_q[np.arange(nsb), NSA_HEAD_DIM + np.arange(nsb)] = 1.0
    place_o = np.zeros((NSA_HPG, NSA_HEAD_DIM, NSA_HPG * NSA_HEAD_DIM), np.float32)
    for h in range(NSA_HPG):
        place_o[h, np.arange(NSA_HEAD_DIM), h * NSA_HEAD_DIM + np.arange(NSA_HEAD_DIM)] = 1.0
    onehot = np.zeros((seq_len, LANES - NSA_HEAD_DIM), np.float32)
    onehot[np.arange(seq_len), np.arange(seq_len) // SEL_BLOCK] = 1.0
    return ov_t, jnp.asarray(place_q, BF16), jnp.asarray(place_o, BF16), jnp.asarray(onehot, BF16)


def _prep_nsa_weights(w_in, cmp_pos, cmp_w1, cmp_w2, w_out):
    d, g, hpg, hd = D_MODEL, NSA_KV_GROUPS, NSA_HPG, NSA_HEAD_DIM
    kv_end = NSA_Q_DIM + 2 * N_BRANCHES * NSA_KV_DIM
    scale = hd ** -0.5 * np.log2(np.e)
    wq = (w_in[:, :NSA_Q_DIM] * scale).reshape(d, NSA_HEADS, hd)
    wq = jnp.pad(wq, ((0, 0), (0, 0), (0, LANES - hd))).reshape(d, NSA_HEADS * LANES)
    w_kvc = w_in[:, NSA_Q_DIM:NSA_Q_DIM + 2 * NSA_KV_DIM]
    w_kvsw = w_in[:, NSA_Q_DIM + 2 * NSA_KV_DIM:kv_end]
    wg = w_in[:, kv_end:].reshape(d, g, hpg * N_BRANCHES)
    wg = jnp.pad(wg, ((0, 0), (0, 0), (0, LANES - hpg * N_BRANCHES))).reshape(d, g * LANES)
    w_all = jnp.concatenate([wq, w_kvc, w_kvsw, wg], axis=1).astype(BF16)
    half = (COMP_BLOCK // 2) * hd
    pos = cmp_pos.reshape(2, 2, 1, half)
    w1a = cmp_w1[:, :half].astype(BF16)
    w1b = cmp_w1[:, half:].astype(BF16)
    return w_all, pos, w1a, w1b, cmp_w2.astype(BF16), w_out.astype(BF16)


def _nsa_layer(h, g_pre, g_post, weights, consts, batch, seq_len):
    w_all, pos, w1a, w1b, w2, w_out = weights
    ov_t, place_q, place_o, onehot = consts
    g, hd = NSA_KV_GROUPS, NSA_HEAD_DIM
    qpad, kvc, kvsw, gates = _norm_proj(
        h, g_pre, w_all,
        [(NSA_HEADS * LANES, BF16), (2 * NSA_KV_DIM, F32), (4 * NSA_KV_DIM, BF16), (g * LANES, F32)])

    per_row = COMP_BLOCK // 2
    a = kvc.reshape(batch, seq_len // per_row, per_row, 2, g, hd).transpose(3, 0, 4, 1, 2, 5)
    a = a.reshape(2, batch * g * CMP_ROWS, per_row * hd)
    kcvc = _compress(a, pos, w1a, w1b, w2)
    kcp = jnp.pad(kcvc[0], ((0, 0), (0, LANES - hd))).astype(BF16).reshape(batch * g, CMP_ROWS, LANES)
    vc = kcvc[1].astype(BF16).reshape(batch * g, CMP_ROWS, hd)
    ocmp, qa = _cmp_sel(qpad, kcp, vc, ov_t, place_q, batch, seq_len)

    kvr = kvsw.reshape(batch, seq_len, 4, g, hd).transpose(2, 0, 3, 1, 4)
    oh = jnp.broadcast_to(onehot, (batch, g, seq_len, LANES - hd))
    ks = jnp.concatenate([kvr[0], oh], axis=-1)
    kw = jnp.pad(kvr[2], ((0, 0), (0, 0), (0, 0), (0, LANES - hd)))
    one_col = jnp.zeros((batch, g, seq_len, LANES - hd), BF16).at[..., 0].set(1.0)
    vs = jnp.concatenate([kvr[1], one_col], axis=-1)
    vw = jnp.concatenate([kvr[3], one_col], axis=-1)
    o = _nsa_attn(qa, ks, vs, kw, vw, ocmp, gates, place_o, batch, seq_len)
    return _out_proj(o, w_out, g_post, h)


def _ssd_in_kernel(x_ref, g_ref, w_ref, cw_ref, dtb_ref, z_ref, xbc_ref, dt_ref, xe_ref, *, tiles_per_seq):
    tm = x_ref.shape[0]
    xn = _rms(x_ref[...], g_ref[...]).astype(BF16)
    _fill_halo(xe_ref, xn, tm, pl.program_id(0) % tiles_per_seq == 0)
    z_ref[...] = _dot(xn, w_ref[:, :SSD_D_INNER]).astype(z_ref.dtype)
    c0, c1 = SSD_D_INNER, SSD_D_INNER + SSD_CONV_DIM
    u = _causal_conv(_dot(xe_ref[...], w_ref[:, c0:c1]), cw_ref[...], SSD_CONV)
    xbc_ref[...] = (u * _sigmoid(u)).astype(xbc_ref.dtype)
    dt_raw = _dot(xn, w_ref[:, c1:]) + dtb_ref[...]
    dt_ref[...] = jnp.maximum(dt_raw, 0.0) + jnp.log1p(jnp.exp(-jnp.abs(dt_raw)))


def _ssd_in(x, gain, w, cw, dtb, seq_len):
    t, d = x.shape
    tm = PROJ_TM
    kern = functools.partial(_ssd_in_kernel, tiles_per_seq=seq_len // tm)
    return pl.pallas_call(
        kern,
        grid=(t // tm,),
        in_specs=[pl.BlockSpec((tm, d), lambda i: (i, 0)),
                  _resident((1, d)), _resident(w.shape), _resident(cw.shape), _resident(dtb.shape)],
        out_specs=[pl.BlockSpec((tm, SSD_D_INNER), lambda i: (i, 0)),
                   pl.BlockSpec((tm, SSD_CONV_DIM), lambda i: (i, 0)),
                   pl.BlockSpec((tm, LANES), lambda i: (i, 0))],
        out_shape=[jax.ShapeDtypeStruct((t, SSD_D_INNER), BF16),
                   jax.ShapeDtypeStruct((t, SSD_CONV_DIM), BF16),
                   jax.ShapeDtypeStruct((t, LANES), F32)],
        scratch_shapes=[pltpu.VMEM((tm + HALO, d), BF16)],
        compiler_params=_cparams("arbitrary"),
        name="ssd_in_proj",
    )(x, gain, w, cw, dtb)


def _ssd_scan_kernel(xbc_ref, z_ref, dt_ref, alog_ref, dskip_ref, nw_ref, tri_ref, y_ref, state_ref):
    L = SSD_CHUNK
    hd = SSD_HEAD_DIM
    gw = SSD_GROUP_WIDTH
    n = SSD_STATE

    @pl.when(pl.program_id(1) == 0)
    def _():
        state_ref[...] = jnp.zeros(state_ref.shape, F32)

    dt = dt_ref[...]
    dta = dt * (-jnp.exp(alog_ref[...]))
    tri = tri_ref[...]
    acs = sum(_dot(tri, part) for part in _split_bf16(dta, 3))
    acs_t = acs.T
    dt_t = dt.T
    last = acs[L - 1:L, :]
    e_in = jnp.exp(acs)
    dec = jnp.exp(last - acs) * dt
    e_last = jnp.exp(last)
    causal = (lax.broadcasted_iota(jnp.int32, (L, L), 0) >= lax.broadcasted_iota(jnp.int32, (L, L), 1))
    lo_half = lax.broadcasted_iota(jnp.int32, (L, LANES), 1) < hd
    lo_row = lo_half[0:1, :]
    zero_bf = jnp.zeros((L, LANES), BF16)

    for g in range(SSD_GROUPS):
        b_g = xbc_ref[:, SSD_D_INNER + g * n:SSD_D_INNER + (g + 1) * n]
        c_g = xbc_ref[:, SSD_D_INNER + (SSD_GROUPS + g) * n:SSD_D_INNER + (SSD_GROUPS + g + 1) * n]
        cb = _dot_nt(c_g, b_g)
        b_t = b_g.astype(F32).T.astype(BF16)
        st = state_ref[g]
        y_inter = _dot(c_g, st.astype(BF16))
        x_g = xbc_ref[:, g * gw:(g + 1) * gw]
        ys, xds, els = [], [], []
        for pr in range(SSD_HEADS_PER_GROUP // 2):
            heads = (g * SSD_HEADS_PER_GROUP + 2 * pr, g * SSD_HEADS_PER_GROUP + 2 * pr + 1)
            xp = x_g[:, pr * LANES:(pr + 1) * LANES]
            ws = []
            for h in heads:
                seg = acs[:, h:h + 1] - acs_t[h:h + 1, :]
                lm = jnp.exp(jnp.where(causal, seg, NEG_INF))
                ws.append((cb * lm * dt_t[h:h + 1, :]).astype(BF16))
            w2 = jnp.concatenate(ws, axis=1)
            x2 = jnp.concatenate([jnp.where(lo_half, xp, zero_bf), jnp.where(lo_half, zero_bf, xp)], axis=0)
            h0, h1 = heads
            e_p = jnp.where(lo_half, e_in[:, h0:h0 + 1], e_in[:, h1:h1 + 1])
            d_p = jnp.where(lo_half, dec[:, h0:h0 + 1], dec[:, h1:h1 + 1])
            ys.append(_dot(w2, x2) + y_inter[:, pr * LANES:(pr + 1) * LANES] * e_p)
            xds.append((xp.astype(F32) * d_p).astype(BF16))
            els.append(jnp.where(lo_row, e_last[:, h0:h0 + 1], e_last[:, h1:h1 + 1]))
        xd = jnp.concatenate(xds, axis=1)
        state_ref[g] = st * jnp.concatenate(els, axis=1) + _dot(b_t, xd)
        sl = slice(g * gw, (g + 1) * gw)
        yg = jnp.concatenate(ys, axis=1) + dskip_ref[:, sl] * x_g.astype(F32)
        zg = z_ref[:, sl].astype(F32)
        yg = yg * (zg * _sigmoid(zg))
        y_ref[:, sl] = _rms(yg, nw_ref[:, sl]).astype(y_ref.dtype)


def _ssd_scan(xbc, z, dt, alog, dskip, nw, tri, batch, seq_len):
    t = xbc.shape[0]
    L = SSD_CHUNK
    nc = seq_len // L
    row = lambda b, c: (b * nc + c, 0)
    return pl.pallas_call(
        _ssd_scan_kernel,
        grid=(batch, nc),
        in_specs=[pl.BlockSpec((L, SSD_CONV_DIM), row),
                  pl.BlockSpec((L, SSD_D_INNER), row),
                  pl.BlockSpec((L, LANES), row),
                  pl.BlockSpec((1, LANES), lambda b, c: (0, 0)),
                  pl.BlockSpec((1, SSD_D_INNER), lambda b, c: (0, 0)),
                  pl.BlockSpec((1, SSD_D_INNER), lambda b, c: (0, 0)),
                  pl.BlockSpec((L, L), lambda b, c: (0, 0))],
        out_specs=pl.BlockSpec((L, SSD_D_INNER), row),
        out_shape=jax.ShapeDtypeStruct((t, SSD_D_INNER), BF16),
        scratch_shapes=[pltpu.VMEM((SSD_GROUPS, SSD_STATE, SSD_GROUP_WIDTH), F32)],
        compiler_params=_cparams("parallel", "arbitrary"),
        name="ssd_scan",
    )(xbc, z, dt, alog, dskip, nw, tri)


def _prep_ssd_weights(w_in, conv_w, conv_b, dt_bias, a_log, d_skip, norm_w, w_out):
    pad = LANES - SSD_HEADS
    w_all = jnp.pad(w_in, ((0, 0), (0, pad))).astype(BF16)
    cw = jnp.concatenate([conv_w, conv_b[None, :], jnp.zeros((8 - SSD_CONV - 1, SSD_CONV_DIM), F32)], axis=0)
    dtb = jnp.pad(dt_bias, (0, pad))[None, :]
    alog = jnp.pad(a_log, (0, pad))[None, :]
    dskip = jnp.repeat(d_skip, SSD_HEAD_DIM)[None, :]
    return w_all, cw, dtb, alog, dskip, norm_w[None, :], w_out.astype(BF16)


def _ssd_layer(h, g_pre, g_post, weights, tri, batch, seq_len):
    w_all, cw, dtb, alog, dskip, nw, w_out = weights
    z, xbc, dt = _ssd_in(h, g_pre, w_all, cw, dtb, seq_len)
    y = _ssd_scan(xbc, z, dt, alog, dskip, nw, tri, batch, seq_len)
    return _out_proj(y, w_out, g_post, h)


def kernel(x, norm_gains, nsa_w_in, nsa_cmp_pos, nsa_cmp_w1, nsa_cmp_w2, nsa_w_out, ssd_w_in, ssd_conv_w, ssd_conv_b, ssd_dt_bias, ssd_a_log, ssd_d, ssd_norm_w, ssd_w_out, ffn_w_up, ffn_conv_w, ffn_conv_b, ffn_w_down):
    batch, seq_len, d = x.shape
    assert d == D_MODEL and seq_len % FFN_TM == 0 and seq_len % ATT_TQ == 0 and seq_len % SSD_CHUNK == 0
    assert seq_len // (COMP_BLOCK // 2) == CMP_ROWS and (batch * NSA_KV_GROUPS) % CMP_SEQ_PER_STEP == 0
    assert ATT_TQ == ATT_TK and WINDOW % ATT_TK == 0 and seq_len // SEL_BLOCK <= LANES - NSA_HEAD_DIM
    h = x.reshape(batch * seq_len, d)
    nsa_consts = _nsa_constants(seq_len)
    tri = jnp.asarray(np.tril(np.ones((SSD_CHUNK, SSD_CHUNK), np.float32)), BF16)
    for i in range(DEPTH):
        gains = norm_gains[i][:, None, :]
        slot = i // N_MIXERS
        if i % N_MIXERS == 0:
            w = _prep_nsa_weights(nsa_w_in[slot], nsa_cmp_pos[slot], nsa_cmp_w1[slot],
                                  nsa_cmp_w2[slot], nsa_w_out[slot])
            h = _nsa_layer(h, gains[0], gains[1], w, nsa_consts, batch, seq_len)
        else:
            w = _prep_ssd_weights(ssd_w_in[slot], ssd_conv_w[slot], ssd_conv_b[slot], ssd_dt_bias[slot],
                                  ssd_a_log[slot], ssd_d[slot], ssd_norm_w[slot], ssd_w_out[slot])
            h = _ssd_layer(h, gains[0], gains[1], w, tri, batch, seq_len)
        fw = _prep_ffn_weights(ffn_w_up[i], ffn_conv_w[i], ffn_conv_b[i], ffn_w_down[i])
        h = _ffn(h, gains[2], *fw, gains[3], seq_len)
    return h.reshape(batch, seq_len, d)
```

```python
import functools

import numpy as np
import jax
import jax.numpy as jnp
from jax import lax
from jax.experimental import pallas as pl
from jax.experimental.pallas import tpu as pltpu

F32 = jnp.float32
BF16 = jnp.bfloat16

D_MODEL = 1024
DEPTH = 4
N_MIXERS = 2

NSA_HEADS = 16
NSA_HEAD_DIM = 64
NSA_KV_GROUPS = 4
NSA_HPG = NSA_HEADS // NSA_KV_GROUPS
COMP_BLOCK = 32
COMP_STRIDE = 16
COMP_HIDDEN = 256
SEL_BLOCK = 64
N_SELECT = 16
N_LOCAL_BLOCKS = 2
WINDOW = 512
N_BRANCHES = 3
NSA_Q_DIM = NSA_HEADS * NSA_HEAD_DIM
NSA_KV_DIM = NSA_KV_GROUPS * NSA_HEAD_DIM

SSD_D_INNER = 2 * D_MODEL
SSD_HEAD_DIM = 64
SSD_HEADS = SSD_D_INNER // SSD_HEAD_DIM
SSD_GROUPS = 4
SSD_STATE = 128
SSD_CONV = 4
SSD_CHUNK = 128
SSD_CONV_DIM = SSD_D_INNER + 2 * SSD_GROUPS * SSD_STATE
SSD_HEADS_PER_GROUP = SSD_HEADS // SSD_GROUPS
SSD_GROUP_WIDTH = SSD_D_INNER // SSD_GROUPS

FFN_HIDDEN = 2816
FFN_CONV = 3

RMS_EPS = 1e-6
NEG_INF = -1e30
SEL_FORCE = 1e9
MASK_BIG = 2.0 ** 100

LANES = 128
HALO = 16
VMEM_LIMIT = 56 * 1024 * 1024

PROJ_TM = 256
OUT_TM = 512
FFN_TM = 512
FFN_FC = 256
ATT_TQ = 256
ATT_TK = 256
CMP_ROWS = 128


def _cparams(*sem):
    return pltpu.CompilerParams(dimension_semantics=sem, vmem_limit_bytes=VMEM_LIMIT)


def _resident(shape):
    nd = len(shape)
    return pl.BlockSpec(shape, lambda *_: (0,) * nd, pipeline_mode=pl.Buffered(1))


def _rms(x, gain):
    ms = jnp.mean(x * x, axis=-1, keepdims=True)
    return x * lax.rsqrt(ms + RMS_EPS) * gain


def _sigmoid(x):
    return 1.0 / (1.0 + jnp.exp(-x))


def _dot(a, b):
    return jnp.dot(a, b, preferred_element_type=F32)


def _dot_nt(a, b):
    return lax.dot_general(a, b, (((1,), (1,)), ((), ())), preferred_element_type=F32)


def _split_bf16(x, parts):
    out = []
    r = x
    for _ in range(parts):
        t = r.astype(BF16)
        out.append(t)
        r = r - t.astype(F32)
    return out


def _causal_conv(u_ext, cw, width):
    y = cw[width - 1:width, :] * u_ext[HALO:, :] + cw[width:width + 1, :]
    for back in range(1, width):
        shifted = pltpu.roll(u_ext, back, axis=0)[HALO:, :]
        y = y + cw[width - 1 - back:width - back, :] * shifted
    return y


def _fill_halo(xe_ref, xn, tm, first_of_seq):
    @pl.when(first_of_seq)
    def _():
        xe_ref[0:HALO, :] = jnp.zeros((HALO, xe_ref.shape[1]), xe_ref.dtype)

    @pl.when(jnp.logical_not(first_of_seq))
    def _():
        xe_ref[0:HALO, :] = xe_ref[tm:tm + HALO, :]

    xe_ref[HALO:, :] = xn


def _out_proj_kernel(y_ref, w_ref, g_ref, h_ref, o_ref):
    f = _dot(y_ref[...], w_ref[...])
    o_ref[...] = h_ref[...] + _rms(f, g_ref[...])


def _out_proj(y, w, gain, h):
    t, k = y.shape
    d = w.shape[1]
    tm = OUT_TM
    return pl.pallas_call(
        _out_proj_kernel,
        grid=(t // tm,),
        in_specs=[pl.BlockSpec((tm, k), lambda i: (i, 0)),
                  _resident(w.shape),
                  _resident((1, d)),
                  pl.BlockSpec((tm, d), lambda i: (i, 0))],
        out_specs=pl.BlockSpec((tm, d), lambda i: (i, 0)),
        out_shape=jax.ShapeDtypeStruct((t, d), F32),
        compiler_params=_cparams("parallel"),
        name="out_proj",
    )(y, w, gain, h)


def _ffn_kernel(x_ref, g1_ref, wv_ref, wg_ref, cv_ref, cg_ref, wd_ref, g2_ref, o_ref, xe_ref,
                *, tiles_per_seq):
    tm = x_ref.shape[0]
    x = x_ref[...]
    xn = _rms(x, g1_ref[...]).astype(BF16)
    _fill_halo(xe_ref, xn, tm, pl.program_id(0) % tiles_per_seq == 0)
    xe = xe_ref[...]

    n_chunks = wv_ref.shape[0]
    f = jnp.zeros((tm, D_MODEL), F32)
    up = (_dot(xe, wv_ref[0]), _dot(xe, wg_ref[0]))
    for c in range(n_chunks):
        uv, ug = up
        if c + 1 < n_chunks:
            up = (_dot(xe, wv_ref[c + 1]), _dot(xe, wg_ref[c + 1]))
        val = _causal_conv(uv, cv_ref[c], FFN_CONV)
        gate = _causal_conv(ug, cg_ref[c], FFN_CONV)
        act = (gate * _sigmoid(gate) * val).astype(BF16)
        f = f + _dot(act, wd_ref[c])
    o_ref[...] = x + _rms(f, g2_ref[...])


def _ffn(x, g1, wv, wg, cv, cg, wd, g2, seq_len):
    t, d = x.shape
    tm = FFN_TM
    kern = functools.partial(_ffn_kernel, tiles_per_seq=seq_len // tm)
    return pl.pallas_call(
        kern,
        grid=(t // tm,),
        in_specs=[pl.BlockSpec((tm, d), lambda i: (i, 0)),
                  _resident((1, d)),
                  _resident(wv.shape), _resident(wg.shape),
                  _resident(cv.shape), _resident(cg.shape),
                  _resident(wd.shape),
                  _resident((1, d))],
        out_specs=pl.BlockSpec((tm, d), lambda i: (i, 0)),
        out_shape=jax.ShapeDtypeStruct((t, d), F32),
        scratch_shapes=[pltpu.VMEM((tm + HALO, d), BF16)],
        compiler_params=_cparams("arbitrary"),
        name="conv_ffn",
    )(x, g1, wv, wg, cv, cg, wd, g2)


def _prep_ffn_weights(w_up, conv_w, conv_b, w_down):
    f, fc = FFN_HIDDEN, FFN_FC
    nch = f // fc

    def chunk_cols(w):
        return w.reshape(w.shape[0], nch, fc).transpose(1, 0, 2)

    def conv_tab(cw, cb):
        tab = jnp.concatenate([cw, cb[None, :], jnp.zeros((8 - FFN_CONV - 1, f), F32)], axis=0)
        return chunk_cols(tab)

    wv = chunk_cols(w_up[:, :f]).astype(BF16)
    wg = chunk_cols(w_up[:, f:]).astype(BF16)
    cv = conv_tab(conv_w[:, :f], conv_b[:f])
    cg = conv_tab(conv_w[:, f:], conv_b[f:])
    wd = w_down.reshape(nch, fc, D_MODEL).astype(BF16)
    return wv, wg, cv, cg, wd


def _nsa_in_kernel(x_ref, g_ref, w_ref, oh_ref, q_ref, ks_ref, vs_ref, kw_ref, vw_ref, kvc_ref, gate_ref):
    g, hpg, hd = NSA_KV_GROUPS, NSA_HPG, NSA_HEAD_DIM
    xn = _rms(x_ref[...], g_ref[...]).astype(BF16)
    res = _dot(xn, w_ref[...])
    slot = lambda k: res[:, k * LANES:(k + 1) * LANES]
    one_col = (lax.broadcasted_iota(jnp.int32, (1, LANES), 1) == hd).astype(F32)
    k = 0
    for gi in range(g):
        for h in range(hpg):
            q_ref[0, gi, h] = slot(k).astype(BF16)
            k += 1
    for gi in range(g):
        ks_ref[0, gi] = (slot(k) + oh_ref[...]).astype(BF16)
        vs_ref[0, gi] = (slot(k + g) + one_col).astype(BF16)
        kw_ref[0, gi] = slot(k + 2 * g).astype(BF16)
        vw_ref[0, gi] = (slot(k + 3 * g) + one_col).astype(BF16)
        k += 1
    k += 3 * g
    kvc_ref[...] = res[:, k * LANES:(k + g) * LANES]
    gate_ref[...] = res[:, (k + g) * LANES:(k + 2 * g) * LANES]


def _nsa_in(x, gain, w, onehot, batch, seq_len):
    t, d = x.shape
    tm = PROJ_TM
    tps = seq_len // tm
    g, hpg = NSA_KV_GROUPS, NSA_HPG
    kv_spec = pl.BlockSpec((1, g, tm, LANES), lambda i: (i // tps, 0, i % tps, 0))
    kv_shape = jax.ShapeDtypeStruct((batch, g, seq_len, LANES), BF16)
    return pl.pallas_call(
        _nsa_in_kernel,
        grid=(t // tm,),
        in_specs=[pl.BlockSpec((tm, d), lambda i: (i, 0)),
                  _resident((1, d)),
                  _resident(w.shape),
                  pl.BlockSpec((tm, LANES), lambda i: (i % tps, 0))],
        out_specs=[pl.BlockSpec((1, g, hpg, tm, LANES), lambda i: (i // tps, 0, 0, i % tps, 0)),
                   kv_spec, kv_spec, kv_spec, kv_spec,
                   pl.BlockSpec((tm, g * LANES), lambda i: (i, 0)),
                   pl.BlockSpec((tm, g * LANES), lambda i: (i, 0))],
        out_shape=[jax.ShapeDtypeStruct((batch, g, hpg, seq_len, LANES), BF16),
                   kv_shape, kv_shape, kv_shape, kv_shape,
                   jax.ShapeDtypeStruct((t, g * LANES), F32),
                   jax.ShapeDtypeStruct((t, g * LANES), F32)],
        compiler_params=_cparams("parallel"),
        name="nsa_in_proj",
    )(x, gain, w, onehot)


def _gelu_tanh(x):
    return 0.5 * x * (1.0 + jnp.tanh(float(np.sqrt(2.0 / np.pi)) * (x + 0.044715 * (x * x * x))))


def _compress_kernel(kv_ref, pos_ref, w1_ref, w2_ref, kc_ref, vc_ref):
    half = COMP_BLOCK // 2
    u = [jnp.zeros((CMP_ROWS, 2 * COMP_HIDDEN), F32) for _ in range(2)]
    for l in range(half):
        x = kv_ref[pl.ds(l, CMP_ROWS, stride=half), :]
        for hf in range(2):
            u[hf] = u[hf] + _dot((x + pos_ref[hf, l]).astype(BF16), w1_ref[hf, l])
    hid = u[0] + pltpu.roll(u[1], CMP_ROWS - 1, axis=0)
    out = _dot(_gelu_tanh(hid).astype(BF16), w2_ref[...])
    kc_ref[0, 0] = out[:, :LANES].astype(BF16)
    vc_ref[0, 0] = out[:, LANES:].astype(BF16)


def _compress(kvc, pos, w1, w2, batch, seq_len):
    g = NSA_KV_GROUPS
    spec = pl.BlockSpec((1, 1, CMP_ROWS, LANES), lambda b, gi: (b, gi, 0, 0))
    shape = jax.ShapeDtypeStruct((batch, g, CMP_ROWS, LANES), BF16)
    return pl.pallas_call(
        _compress_kernel,
        grid=(batch, g),
        in_specs=[pl.BlockSpec((seq_len, LANES), lambda b, gi: (b, gi)),
                  pl.BlockSpec(pos.shape, lambda b, gi: (0, 0, 0, 0)),
                  pl.BlockSpec(w1.shape, lambda b, gi: (0, 0, 0, 0)),
                  pl.BlockSpec(w2.shape, lambda b, gi: (0, 0))],
        out_specs=[spec, spec],
        out_shape=[shape, shape],
        compiler_params=_cparams("parallel", "parallel"),
        name="nsa_compress",
    )(kvc, pos, w1, w2)


def _softmax_pv(s, v, m, acc):
    m_new = jnp.maximum(m, jnp.max(s, axis=-1, keepdims=True))
    p = jnp.exp2(s - m_new).astype(BF16)
    return m_new, jnp.exp2(m - m_new) * acc + _dot(p, v)


def _select_bias(psum, ov_ref, q0, tq):
    nsb = ov_ref.shape[0]
    hi, lo = _split_bf16(psum, 2)
    pslc = _dot(ov_ref[...], hi) + _dot(ov_ref[...], lo)
    j_io = lax.broadcasted_iota(jnp.int32, (nsb, tq), 0)
    pos = q0 + lax.broadcasted_iota(jnp.int32, (nsb, tq), 1)
    cur = jnp.right_shift(pos, SEL_BLOCK.bit_length() - 1)
    allowed = j_io * SEL_BLOCK <= pos
    forced = (j_io == 0) | ((j_io <= cur) & (j_io > cur - N_LOCAL_BLOCKS))
    score = jnp.where(forced, SEL_FORCE, jnp.where(allowed, pslc, NEG_INF))
    rank = jnp.zeros((nsb, tq), F32)
    for i in range(nsb):
        row = score[i:i + 1, :]
        before = (row > score) | ((row == score) & (j_io > i))
        rank = rank + jnp.where(before, 1.0, 0.0)
    return jnp.where(rank < float(min(N_SELECT, nsb)), 0.0, -MASK_BIG)


def _nsa_attn_kernel(q_ref, ks_ref, vs_ref, kw_ref, vw_ref, kc_ref, vc_ref, gate_ref, ov_ref, pq_ref, po_ref,
                     o_ref):
    hpg, hd = NSA_HPG, NSA_HEAD_DIM
    tq = q_ref.shape[3]
    tk = ATT_TK
    i = pl.program_id(2)
    q0 = i * tq
    rows = hpg * tq
    q_raw = q_ref[0, 0].reshape(rows, LANES)

    def tile(ref, j, n):
        return ref[0, 0, pl.ds(pl.multiple_of(j * tk, tk), n), :]

    nwk = WINDOW + tk
    j0 = jnp.maximum(i - WINDOW // tk, 0)
    s = _dot_nt(q_raw, tile(kw_ref, j0, nwk))
    qpos = q0 + (lax.broadcasted_iota(jnp.int32, (rows, nwk), 0) & (tq - 1))
    kpos = j0 * tk + lax.broadcasted_iota(jnp.int32, (rows, nwk), 1)
    s = jnp.where((kpos <= qpos) & (kpos > qpos - WINDOW), s, NEG_INF)
    p = jnp.exp2(s - jnp.max(s, axis=-1, keepdims=True)).astype(BF16)
    acc = _dot(p, tile(vw_ref, j0, nwk))
    o_win = acc * (1.0 / acc[:, hd:hd + 1])

    kc = kc_ref[0, 0]
    vc = vc_ref[0, 0]
    c_io = lax.broadcasted_iota(jnp.int32, (CMP_ROWS, tq), 0)
    pos_c = q0 + lax.broadcasted_iota(jnp.int32, (CMP_ROWS, tq), 1)
    cmask = c_io * COMP_STRIDE + (COMP_BLOCK - 1) <= pos_c
    psum = jnp.zeros((CMP_ROWS, tq), F32)
    o_cmp = []
    for h in range(hpg):
        st = jnp.where(cmask, _dot_nt(kc, q_ref[0, 0, h]), NEG_INF)
        e = jnp.where(cmask, jnp.exp2(st - jnp.max(st, axis=0, keepdims=True)), 0.0)
        den = jnp.sum(e, axis=0, keepdims=True)
        pc = e / jnp.where(den > 0.0, den, 1.0)
        psum = psum + pc
        o_cmp.append(_dot(pc.T.astype(BF16), vc))

    bias = _select_bias(psum, ov_ref, q0, tq)
    bias = jnp.concatenate([bias, jnp.zeros((LANES - bias.shape[0], tq), F32)], axis=0)
    placed = _dot(bias.T.astype(BF16), pq_ref[...])
    q_sel = jnp.concatenate([(q_ref[0, 0, h].astype(F32) + placed).astype(BF16) for h in range(hpg)], axis=0)

    def sel_body(j, carry):
        s, m, acc = carry
        s_next = _dot_nt(q_sel, tile(ks_ref, j + 1, tk))
        m, acc = _softmax_pv(s, tile(vs_ref, j, tk), m, acc)
        return s_next, m, acc

    init = (_dot_nt(q_sel, tile(ks_ref, 0, tk)), jnp.full((rows, 1), NEG_INF, F32),
            jnp.zeros((rows, LANES), F32))
    s, m, acc = lax.fori_loop(0, i, sel_body, init)
    r_io = lax.broadcasted_iota(jnp.int32, (rows, tk), 0) & (tq - 1)
    c_io = lax.broadcasted_iota(jnp.int32, (rows, tk), 1)
    s = jnp.where(c_io <= r_io, s, NEG_INF)
    _, acc = _softmax_pv(s, tile(vs_ref, i, tk), m, acc)
    o_sel = acc * (1.0 / acc[:, hd:hd + 1])

    gates = _sigmoid(gate_ref[...])
    out = jnp.zeros((tq, hpg * hd), F32)
    for h in range(hpg):
        gb = h * N_BRANCHES
        rs = slice(h * tq, (h + 1) * tq)
        o_h = (gates[:, gb:gb + 1] * o_cmp[h]
               + gates[:, gb + 1:gb + 2] * o_sel[rs]
               + gates[:, gb + 2:gb + 3] * o_win[rs])
        out = out + _dot(o_h.astype(BF16), po_ref[h])
    o_ref[...] = out.astype(BF16)


def _nsa_attn(q, ks, vs, kw, vw, kc, vc, gates, ov_t, place_q, place_o, batch, seq_len):
    t = batch * seq_len
    tq = ATT_TQ
    nq = seq_len // tq
    g, hpg, hd = NSA_KV_GROUPS, NSA_HPG, NSA_HEAD_DIM
    kvspec = pl.BlockSpec((1, 1, seq_len, LANES), lambda b, gi, i: (b, gi, 0, 0))
    cspec = pl.BlockSpec((1, 1, CMP_ROWS, LANES), lambda b, gi, i: (b, gi, 0, 0))
    return pl.pallas_call(
        _nsa_attn_kernel,
        grid=(batch, g, nq),
        in_specs=[pl.BlockSpec((1, 1, hpg, tq, LANES), lambda b, gi, i: (b, gi, 0, i, 0)),
                  kvspec, kvspec, kvspec, kvspec, cspec, cspec,
                  pl.BlockSpec((tq, LANES), lambda b, gi, i: (b * nq + i, gi)),
                  pl.BlockSpec(ov_t.shape, lambda b, gi, i: (0, 0)),
                  pl.BlockSpec(place_q.shape, lambda b, gi, i: (0, 0)),
                  pl.BlockSpec(place_o.shape, lambda b, gi, i: (0, 0, 0))],
        out_specs=pl.BlockSpec((tq, hpg * hd), lambda b, gi, i: (b * nq + i, gi)),
        out_shape=jax.ShapeDtypeStruct((t, g * hpg * hd), BF16),
        compiler_params=_cparams("parallel", "parallel", "arbitrary"),
        name="nsa_attention",
    )(q, ks, vs, kw, vw, kc, vc, gates, ov_t, place_q, place_o)


def _nsa_constants(seq_len):
    nsb = seq_len // SEL_BLOCK
    nc = CMP_ROWS
    cs = np.arange(nc) * COMP_STRIDE
    ss = np.arange(nsb) * SEL_BLOCK
    ov = (np.minimum(cs[:, None] + COMP_BLOCK, ss[None, :] + SEL_BLOCK)
          - np.maximum(cs[:, None], ss[None, :]))
    ov = np.clip(ov, 0, None).astype(np.float32) / COMP_BLOCK
    ov[(seq_len - COMP_BLOCK) // COMP_STRIDE + 1:, :] = 0.0
    ov_t = jnp.asarray(ov.T, BF16)
    place_q = np.zeros((LANES, LANES), np.float32)
    place_q[np.arange(nsb), NSA_HEAD_DIM + np.arange(nsb)] = 1.0
    place_o = np.zeros((NSA_HPG, LANES, NSA_HPG * NSA_HEAD_DIM), np.float32)
    for h in range(NSA_HPG):
        place_o[h, np.arange(NSA_HEAD_DIM), h * NSA_HEAD_DIM + np.arange(NSA_HEAD_DIM)] = 1.0
    onehot = np.zeros((seq_len, LANES), np.float32)
    onehot[np.arange(seq_len), NSA_HEAD_DIM + np.arange(seq_len) // SEL_BLOCK] = 1.0
    return ov_t, jnp.asarray(place_q, BF16), jnp.asarray(place_o, BF16), jnp.asarray(onehot, F32)


def _prep_nsa_weights(w_in, cmp_pos, cmp_w1, cmp_w2, w_out):
    d, g, hpg, hd = D_MODEL, NSA_KV_GROUPS, NSA_HPG, NSA_HEAD_DIM
    kv_end = NSA_Q_DIM + 2 * N_BRANCHES * NSA_KV_DIM
    half = COMP_BLOCK // 2
    hid = COMP_HIDDEN

    def lane_slots(w, n):
        return jnp.pad(w.reshape(d, n, hd), ((0, 0), (0, 0), (0, LANES - hd))).reshape(d, n * LANES)

    scale = hd ** -0.5 * np.log2(np.e)
    wq = lane_slots(w_in[:, :NSA_Q_DIM] * scale, NSA_HEADS)
    kv = w_in[:, NSA_Q_DIM:kv_end].reshape(d, 2 * N_BRANCHES, g, hd)
    w_sw = [lane_slots(kv[:, br].reshape(d, g * hd), g) for br in (2, 3, 4, 5)]
    w_kvc = jnp.concatenate([kv[:, 0], kv[:, 1]], axis=-1).reshape(d, g * LANES)
    wg = w_in[:, kv_end:].reshape(d, g, hpg * N_BRANCHES)
    wg = jnp.pad(wg, ((0, 0), (0, 0), (0, LANES - hpg * N_BRANCHES))).reshape(d, g * LANES)
    w_all = jnp.concatenate([wq] + w_sw + [w_kvc, wg], axis=1).astype(BF16)

    w1 = cmp_w1.reshape(2, 2, half, hd, hid)
    zeros = jnp.zeros_like(w1[0])
    w1bd = jnp.concatenate([jnp.concatenate([w1[0], zeros], axis=-1),
                            jnp.concatenate([zeros, w1[1]], axis=-1)], axis=-2).astype(BF16)
    pos = cmp_pos.reshape(2, 2, half, 1, hd)
    posbd = jnp.concatenate([pos[0], pos[1]], axis=-1)
    w2bd = jnp.zeros((2 * hid, 2 * LANES), F32)
    w2bd = w2bd.at[:hid, :hd].set(cmp_w2[0]).at[hid:, LANES:LANES + hd].set(cmp_w2[1]).astype(BF16)
    return w_all, posbd, w1bd, w2bd, w_out.astype(BF16)


def _nsa_layer(h, g_pre, g_post, weights, consts, batch, seq_len):
    w_all, posbd, w1bd, w2bd, w_out = weights
    ov_t, place_q, place_o, onehot = consts
    q, ks, vs, kw, vw, kvc, gates = _nsa_in(h, g_pre, w_all, onehot, batch, seq_len)
    kc, vc = _compress(kvc, posbd, w1bd, w2bd, batch, seq_len)
    o = _nsa_attn(q, ks, vs, kw, vw, kc, vc, gates, ov_t, place_q, place_o, batch, seq_len)
    return _out_proj(o, w_out, g_post, h)


def _ssd_in_kernel(x_ref, g_ref, w_ref, cw_ref, dtb_ref, z_ref, xbc_ref, dt_ref, xe_ref, *, tiles_per_seq):
    tm = x_ref.shape[0]
    xn = _rms(x_ref[...], g_ref[...]).astype(BF16)
    _fill_halo(xe_ref, xn, tm, pl.program_id(0) % tiles_per_seq == 0)
    z_ref[...] = _dot(xn, w_ref[:, :SSD_D_INNER]).astype(z_ref.dtype)
    c0, c1 = SSD_D_INNER, SSD_D_INNER + SSD_CONV_DIM
    u = _causal_conv(_dot(xe_ref[...], w_ref[:, c0:c1]), cw_ref[...], SSD_CONV)
    xbc_ref[...] = (u * _sigmoid(u)).astype(xbc_ref.dtype)
    dt_raw = _dot(xn, w_ref[:, c1:]) + dtb_ref[...]
    dt_ref[...] = jnp.maximum(dt_raw, 0.0) + jnp.log1p(jnp.exp(-jnp.abs(dt_raw)))


def _ssd_in(x, gain, w, cw, dtb, seq_len):
    t, d = x.shape
    tm = PROJ_TM
    kern = functools.partial(_ssd_in_kernel, tiles_per_seq=seq_len // tm)
    return pl.pallas_call(
        kern,
        grid=(t // tm,),
        in_specs=[pl.BlockSpec((tm, d), lambda i: (i, 0)),
                  _resident((1, d)), _resident(w.shape), _resident(cw.shape), _resident(dtb.shape)],
        out_specs=[pl.BlockSpec((tm, SSD_D_INNER), lambda i: (i, 0)),
                   pl.BlockSpec((tm, SSD_CONV_DIM), lambda i: (i, 0)),
                   pl.BlockSpec((tm, LANES), lambda i: (i, 0))],
        out_shape=[jax.ShapeDtypeStruct((t, SSD_D_INNER), BF16),
                   jax.ShapeDtypeStruct((t, SSD_CONV_DIM), BF16),
                   jax.ShapeDtypeStruct((t, LANES), F32)],
        scratch_shapes=[pltpu.VMEM((tm + HALO, d), BF16)],
        compiler_params=_cparams("arbitrary"),
        name="ssd_in_proj",
    )(x, gain, w, cw, dtb)


def _ssd_scan_kernel(xbc_ref, z_ref, dt_ref, alog_ref, dskip_ref, nw_ref, tri_ref, y_ref, state_ref):
    L = SSD_CHUNK
    hd = SSD_HEAD_DIM
    gw = SSD_GROUP_WIDTH
    n = SSD_STATE

    @pl.when(pl.program_id(1) == 0)
    def _():
        state_ref[...] = jnp.zeros(state_ref.shape, F32)

    dt = dt_ref[...]
    dta = dt * (-jnp.exp(alog_ref[...]))
    tri = tri_ref[...]
    acs = sum(_dot(tri, part) for part in _split_bf16(dta, 3))
    acs_t = acs.T
    dt_t = dt.T
    last = acs[L - 1:L, :]
    e_in = jnp.exp(acs)
    dec = jnp.exp(last - acs) * dt
    e_last = jnp.exp(last)
    causal = (lax.broadcasted_iota(jnp.int32, (L, L), 0) >= lax.broadcasted_iota(jnp.int32, (L, L), 1))
    lo_half = lax.broadcasted_iota(jnp.int32, (L, LANES), 1) < hd
    lo_row = lo_half[0:1, :]
    zero_bf = jnp.zeros((L, LANES), BF16)

    for g in range(SSD_GROUPS):
        b_g = xbc_ref[:, SSD_D_INNER + g * n:SSD_D_INNER + (g + 1) * n]
        c_g = xbc_ref[:, SSD_D_INNER + (SSD_GROUPS + g) * n:SSD_D_INNER + (SSD_GROUPS + g + 1) * n]
        cb = _dot_nt(c_g, b_g)
        b_t = b_g.astype(F32).T.astype(BF16)
        st = state_ref[g]
        y_inter = _dot(c_g, st.astype(BF16))
        x_g = xbc_ref[:, g * gw:(g + 1) * gw]
        ys, xds, els = [], [], []
        for pr in range(SSD_HEADS_PER_GROUP // 2):
            heads = (g * SSD_HEADS_PER_GROUP + 2 * pr, g * SSD_HEADS_PER_GROUP + 2 * pr + 1)
            xp = x_g[:, pr * LANES:(pr + 1) * LANES]
            ws = []
            for h in heads:
                seg = acs[:, h:h + 1] - acs_t[h:h + 1, :]
                lm = jnp.exp(jnp.where(causal, seg, NEG_INF))
                ws.append((cb * lm * dt_t[h:h + 1, :]).astype(BF16))
            w2 = jnp.concatenate(ws, axis=1)
            x2 = jnp.concatenate([jnp.where(lo_half, xp, zero_bf), jnp.where(lo_half, zero_bf, xp)], axis=0)
            h0, h1 = heads
            e_p = jnp.where(lo_half, e_in[:, h0:h0 + 1], e_in[:, h1:h1 + 1])
            d_p = jnp.where(lo_half, dec[:, h0:h0 + 1], dec[:, h1:h1 + 1])
            ys.append(_dot(w2, x2) + y_inter[:, pr * LANES:(pr + 1) * LANES] * e_p)
            xds.append((xp.astype(F32) * d_p).astype(BF16))
            els.append(jnp.where(lo_row, e_last[:, h0:h0 + 1], e_last[:, h1:h1 + 1]))
        xd = jnp.concatenate(xds, axis=1)
        state_ref[g] = st * jnp.concatenate(els, axis=1) + _dot(b_t, xd)
        sl = slice(g * gw, (g + 1) * gw)
        yg = jnp.concatenate(ys, axis=1) + dskip_ref[:, sl] * x_g.astype(F32)
        zg = z_ref[:, sl].astype(F32)
        yg = yg * (zg * _sigmoid(zg))
        y_ref[:, sl] = _rms(yg, nw_ref[:, sl]).astype(y_ref.dtype)


def _ssd_scan(xbc, z, dt, alog, dskip, nw, tri, batch, seq_len):
    t = xbc.shape[0]
    L = SSD_CHUNK
    nc = seq_len // L
    row = lambda b, c: (b * nc + c, 0)
    return pl.pallas_call(
        _ssd_scan_kernel,
        grid=(batch, nc),
        in_specs=[pl.BlockSpec((L, SSD_CONV_DIM), row),
                  pl.BlockSpec((L, SSD_D_INNER), row),
                  pl.BlockSpec((L, LANES), row),
                  pl.BlockSpec((1, LANES), lambda b, c: (0, 0)),
                  pl.BlockSpec((1, SSD_D_INNER), lambda b, c: (0, 0)),
                  pl.BlockSpec((1, SSD_D_INNER), lambda b, c: (0, 0)),
                  pl.BlockSpec((L, L), lambda b, c: (0, 0))],
        out_specs=pl.BlockSpec((L, SSD_D_INNER), row),
        out_shape=jax.ShapeDtypeStruct((t, SSD_D_INNER), BF16),
        scratch_shapes=[pltpu.VMEM((SSD_GROUPS, SSD_STATE, SSD_GROUP_WIDTH), F32)],
        compiler_params=_cparams("parallel", "arbitrary"),
        name="ssd_scan",
    )(xbc, z, dt, alog, dskip, nw, tri)


def _prep_ssd_weights(w_in, conv_w, conv_b, dt_bias, a_log, d_skip, norm_w, w_out):
    pad = LANES - SSD_HEADS
    w_all = jnp.pad(w_in, ((0, 0), (0, pad))).astype(BF16)
    cw = jnp.concatenate([conv_w, conv_b[None, :], jnp.zeros((8 - SSD_CONV - 1, SSD_CONV_DIM), F32)], axis=0)
    dtb = jnp.pad(dt_bias, (0, pad))[None, :]
    alog = jnp.pad(a_log, (0, pad))[None, :]
    dskip = jnp.repeat(d_skip, SSD_HEAD_DIM)[None, :]
    return w_all, cw, dtb, alog, dskip, norm_w[None, :], w_out.astype(BF16)


def _ssd_layer(h, g_pre, g_post, weights, tri, batch, seq_len):
    w_all, cw, dtb, alog, dskip, nw, w_out = weights
    z, xbc, dt = _ssd_in(h, g_pre, w_all, cw, dtb, seq_len)
    y = _ssd_scan(xbc, z, dt, alog, dskip, nw, tri, batch, seq_len)
    return _out_proj(y, w_out, g_post, h)


def kernel(x, norm_gains, nsa_w_in, nsa_cmp_pos, nsa_cmp_w1, nsa_cmp_w2, nsa_w_out, ssd_w_in, ssd_conv_w, ssd_conv_b, ssd_dt_bias, ssd_a_log, ssd_d, ssd_norm_w, ssd_w_out, ffn_w_up, ffn_conv_w, ffn_conv_b, ffn_w_down):
    batch, seq_len, d = x.shape
    assert d == D_MODEL and seq_len % FFN_TM == 0 and seq_len % ATT_TQ == 0 and seq_len % SSD_CHUNK == 0
    assert seq_len // (COMP_BLOCK // 2) == CMP_ROWS and seq_len % PROJ_TM == 0 and seq_len >= WINDOW + ATT_TK
    assert ATT_TQ == ATT_TK and WINDOW % ATT_TK == 0 and seq_len // SEL_BLOCK <= LANES - NSA_HEAD_DIM
    h = x.reshape(batch * seq_len, d)
    nsa_consts = _nsa_constants(seq_len)
    tri = jnp.asarray(np.tril(np.ones((SSD_CHUNK, SSD_CHUNK), np.float32)), BF16)
    for i in range(DEPTH):
        gains = norm_gains[i][:, None, :]
        slot = i // N_MIXERS
        if i % N_MIXERS == 0:
            w = _prep_nsa_weights(nsa_w_in[slot], nsa_cmp_pos[slot], nsa_cmp_w1[slot],
                                  nsa_cmp_w2[slot], nsa_w_out[slot])
            h = _nsa_layer(h, gains[0], gains[1], w, nsa_consts, batch, seq_len)
        else:
            w = _prep_ssd_weights(ssd_w_in[slot], ssd_conv_w[slot], ssd_conv_b[slot], ssd_dt_bias[slot],
                                  ssd_a_log[slot], ssd_d[slot], ssd_norm_w[slot], ssd_w_out[slot])
            h = _ssd_layer(h, gains[0], gains[1], w, tri, batch, seq_len)
        fw = _prep_ffn_weights(ffn_w_up[i], ffn_conv_w[i], ffn_conv_b[i], ffn_w_down[i])
        h = _ffn(h, gains[2], *fw, gains[3], seq_len)
    return h.reshape(batch, seq_len, d)
```

```python
import functools

import numpy as np
import jax
import jax.numpy as jnp
from jax import lax
from jax.experimental import pallas as pl
from jax.experimental.pallas import tpu as pltpu

F32 = jnp.float32
BF16 = jnp.bfloat16

D_MODEL = 1024
DEPTH = 4
N_MIXERS = 2

NSA_HEADS = 16
NSA_HEAD_DIM = 64
NSA_KV_GROUPS = 4
NSA_HPG = NSA_HEADS // NSA_KV_GROUPS
COMP_BLOCK = 32
COMP_STRIDE = 16
COMP_HIDDEN = 256
SEL_BLOCK = 64
N_SELECT = 16
N_LOCAL_BLOCKS = 2
WINDOW = 512
N_BRANCHES = 3
NSA_Q_DIM = NSA_HEADS * NSA_HEAD_DIM
NSA_KV_DIM = NSA_KV_GROUPS * NSA_HEAD_DIM

SSD_D_INNER = 2 * D_MODEL
SSD_HEAD_DIM = 64
SSD_HEADS = SSD_D_INNER // SSD_HEAD_DIM
SSD_GROUPS = 4
SSD_STATE = 128
SSD_CONV = 4
SSD_CHUNK = 128
SSD_CONV_DIM = SSD_D_INNER + 2 * SSD_GROUPS * SSD_STATE
SSD_HEADS_PER_GROUP = SSD_HEADS // SSD_GROUPS
SSD_GROUP_WIDTH = SSD_D_INNER // SSD_GROUPS

FFN_HIDDEN = 2816
FFN_CONV = 3

RMS_EPS = 1e-6
NEG_INF = -1e30
SEL_FORCE = 1e9
MASK_BIG = 2.0 ** 100

LANES = 128
HALO = 16
VMEM_LIMIT = 56 * 1024 * 1024

PROJ_TM = 256
OUT_TM = 512
FFN_TM = 512
FFN_FC = 256
ATT_TQ = 256
ATT_TK = 256
CMP_ROWS = 128


def _cparams(*sem):
    return pltpu.CompilerParams(dimension_semantics=sem, vmem_limit_bytes=VMEM_LIMIT)


def _resident(shape):
    nd = len(shape)
    return pl.BlockSpec(shape, lambda *_: (0,) * nd, pipeline_mode=pl.Buffered(1))


def _rms(x, gain):
    ms = jnp.mean(x * x, axis=-1, keepdims=True)
    return x * lax.rsqrt(ms + RMS_EPS) * gain


def _sigmoid(x):
    return 1.0 / (1.0 + jnp.exp(-x))


def _dot(a, b):
    return jnp.dot(a, b, preferred_element_type=F32)


def _dot_nt(a, b):
    return lax.dot_general(a, b, (((1,), (1,)), ((), ())), preferred_element_type=F32)


def _split_bf16(x, parts):
    out = []
    r = x
    for _ in range(parts):
        t = r.astype(BF16)
        out.append(t)
        r = r - t.astype(F32)
    return out


def _causal_conv(u_ext, cw, width):
    y = cw[width - 1:width, :] * u_ext[HALO:, :] + cw[width:width + 1, :]
    for back in range(1, width):
        shifted = pltpu.roll(u_ext, back, axis=0)[HALO:, :]
        y = y + cw[width - 1 - back:width - back, :] * shifted
    return y


def _fill_halo(xe_ref, xn, tm, first_of_seq):
    @pl.when(first_of_seq)
    def _():
        xe_ref[0:HALO, :] = jnp.zeros((HALO, xe_ref.shape[1]), xe_ref.dtype)

    @pl.when(jnp.logical_not(first_of_seq))
    def _():
        xe_ref[0:HALO, :] = xe_ref[tm:tm + HALO, :]

    xe_ref[HALO:, :] = xn


def _out_proj_kernel(y_ref, w_ref, g_ref, h_ref, o_ref):
    f = _dot(y_ref[...], w_ref[...])
    o_ref[...] = h_ref[...] + _rms(f, g_ref[...])


def _out_proj(y, w, gain, h):
    t, k = y.shape
    d = w.shape[1]
    tm = OUT_TM
    return pl.pallas_call(
        _out_proj_kernel,
        grid=(t // tm,),
        in_specs=[pl.BlockSpec((tm, k), lambda i: (i, 0)),
                  _resident(w.shape),
                  _resident((1, d)),
                  pl.BlockSpec((tm, d), lambda i: (i, 0))],
        out_specs=pl.BlockSpec((tm, d), lambda i: (i, 0)),
        out_shape=jax.ShapeDtypeStruct((t, d), F32),
        compiler_params=_cparams("parallel"),
        name="out_proj",
    )(y, w, gain, h)


def _ffn_kernel(x_ref, g1_ref, wv_ref, wg_ref, cv_ref, cg_ref, wd_ref, g2_ref, o_ref, xe_ref,
                *, tiles_per_seq):
    tm = x_ref.shape[0]
    x = x_ref[...]
    xn = _rms(x, g1_ref[...]).astype(BF16)
    _fill_halo(xe_ref, xn, tm, pl.program_id(0) % tiles_per_seq == 0)
    xe = xe_ref[...]

    n_chunks = wv_ref.shape[0]
    f = jnp.zeros((tm, D_MODEL), F32)
    up = (_dot(xe, wv_ref[0]), _dot(xe, wg_ref[0]))
    for c in range(n_chunks):
        uv, ug = up
        if c + 1 < n_chunks:
            up = (_dot(xe, wv_ref[c + 1]), _dot(xe, wg_ref[c + 1]))
        val = _causal_conv(uv, cv_ref[c], FFN_CONV)
        gate = _causal_conv(ug, cg_ref[c], FFN_CONV)
        act = (gate * _sigmoid(gate) * val).astype(BF16)
        f = f + _dot(act, wd_ref[c])
    o_ref[...] = x + _rms(f, g2_ref[...])


def _ffn(x, g1, wv, wg, cv, cg, wd, g2, seq_len):
    t, d = x.shape
    tm = FFN_TM
    kern = functools.partial(_ffn_kernel, tiles_per_seq=seq_len // tm)
    return pl.pallas_call(
        kern,
        grid=(t // tm,),
        in_specs=[pl.BlockSpec((tm, d), lambda i: (i, 0)),
                  _resident((1, d)),
                  _resident(wv.shape), _resident(wg.shape),
                  _resident(cv.shape), _resident(cg.shape),
                  _resident(wd.shape),
                  _resident((1, d))],
        out_specs=pl.BlockSpec((tm, d), lambda i: (i, 0)),
        out_shape=jax.ShapeDtypeStruct((t, d), F32),
        scratch_shapes=[pltpu.VMEM((tm + HALO, d), BF16)],
        compiler_params=_cparams("arbitrary"),
        name="conv_ffn",
    )(x, g1, wv, wg, cv, cg, wd, g2)


def _prep_ffn_weights(w_up, conv_w, conv_b, w_down):
    f, fc = FFN_HIDDEN, FFN_FC
    nch = f // fc

    def chunk_cols(w):
        return w.reshape(w.shape[0], nch, fc).transpose(1, 0, 2)

    def conv_tab(cw, cb):
        tab = jnp.concatenate([cw, cb[None, :], jnp.zeros((8 - FFN_CONV - 1, f), F32)], axis=0)
        return chunk_cols(tab)

    wv = chunk_cols(w_up[:, :f]).astype(BF16)
    wg = chunk_cols(w_up[:, f:]).astype(BF16)
    cv = conv_tab(conv_w[:, :f], conv_b[:f])
    cg = conv_tab(conv_w[:, f:], conv_b[f:])
    wd = w_down.reshape(nch, fc, D_MODEL).astype(BF16)
    return wv, wg, cv, cg, wd


def _nsa_in_kernel(x_ref, g_ref, w_ref, oh_ref, qt_ref, ks_ref, vst_ref, kw_ref, vwt_ref, kvc_ref, gate_ref):
    g, hpg, hd = NSA_KV_GROUPS, NSA_HPG, NSA_HEAD_DIM
    xn = _rms(x_ref[...], g_ref[...]).astype(BF16)
    res = _dot(xn, w_ref[...])
    slot = lambda k: res[:, k * LANES:(k + 1) * LANES]
    one_col = (lax.broadcasted_iota(jnp.int32, (1, LANES), 1) == hd).astype(F32)
    k = 0
    for gi in range(g):
        for h in range(hpg):
            qt_ref[0, gi, h] = slot(k).T.astype(BF16)
            k += 1
    for gi in range(g):
        ks_ref[0, gi] = (slot(k) + oh_ref[...]).astype(BF16)
        vst_ref[0, gi, 0] = (slot(k + g) + one_col).T.astype(BF16)
        kw_ref[0, gi] = slot(k + 2 * g).astype(BF16)
        vwt_ref[0, gi, 0] = (slot(k + 3 * g) + one_col).T.astype(BF16)
        k += 1
    k += 3 * g
    kvc_ref[...] = res[:, k * LANES:(k + g) * LANES]
    gate_ref[...] = res[:, (k + g) * LANES:(k + 2 * g) * LANES]


def _nsa_in(x, gain, w, onehot, batch, seq_len):
    t, d = x.shape
    tm = PROJ_TM
    tps = seq_len // tm
    g, hpg = NSA_KV_GROUPS, NSA_HPG
    k_spec = pl.BlockSpec((1, g, tm, LANES), lambda i: (i // tps, 0, i % tps, 0))
    k_shape = jax.ShapeDtypeStruct((batch, g, seq_len, LANES), BF16)
    vt_spec = pl.BlockSpec((1, g, 1, LANES, tm), lambda i: (i // tps, 0, i % tps, 0, 0))
    vt_shape = jax.ShapeDtypeStruct((batch, g, tps, LANES, tm), BF16)
    return pl.pallas_call(
        _nsa_in_kernel,
        grid=(t // tm,),
        in_specs=[pl.BlockSpec((tm, d), lambda i: (i, 0)),
                  _resident((1, d)),
                  _resident(w.shape),
                  pl.BlockSpec((tm, LANES), lambda i: (i % tps, 0))],
        out_specs=[pl.BlockSpec((1, g, hpg, LANES, tm), lambda i: (i // tps, 0, 0, 0, i % tps)),
                   k_spec, vt_spec, k_spec, vt_spec,
                   pl.BlockSpec((tm, g * LANES), lambda i: (i, 0)),
                   pl.BlockSpec((tm, g * LANES), lambda i: (i, 0))],
        out_shape=[jax.ShapeDtypeStruct((batch, g, hpg, LANES, seq_len), BF16),
                   k_shape, vt_shape, k_shape, vt_shape,
                   jax.ShapeDtypeStruct((t, g * LANES), F32),
                   jax.ShapeDtypeStruct((t, g * LANES), F32)],
        compiler_params=_cparams("parallel"),
        name="nsa_in_proj",
    )(x, gain, w, onehot)


def _gelu_tanh(x):
    return 0.5 * x * (1.0 + jnp.tanh(float(np.sqrt(2.0 / np.pi)) * (x + 0.044715 * (x * x * x))))


def _compress_kernel(kv_ref, pos_ref, w1_ref, w2_ref, kc_ref, vc_ref):
    half = COMP_BLOCK // 2
    u = [jnp.zeros((CMP_ROWS, 2 * COMP_HIDDEN), F32) for _ in range(2)]
    for l in range(half):
        x = kv_ref[pl.ds(l, CMP_ROWS, stride=half), :]
        for hf in range(2):
            u[hf] = u[hf] + _dot((x + pos_ref[hf, l]).astype(BF16), w1_ref[hf, l])
    hid = u[0] + pltpu.roll(u[1], CMP_ROWS - 1, axis=0)
    out = _dot(_gelu_tanh(hid).astype(BF16), w2_ref[...])
    kc_ref[0, 0] = out[:, :LANES].astype(BF16)
    vc_ref[0, 0] = out[:, LANES:].T.astype(BF16)


def _compress(kvc, pos, w1, w2, batch, seq_len):
    g = NSA_KV_GROUPS
    spec = pl.BlockSpec((1, 1, CMP_ROWS, LANES), lambda b, gi: (b, gi, 0, 0))
    shape = jax.ShapeDtypeStruct((batch, g, CMP_ROWS, LANES), BF16)
    return pl.pallas_call(
        _compress_kernel,
        grid=(batch, g),
        in_specs=[pl.BlockSpec((seq_len, LANES), lambda b, gi: (b, gi)),
                  pl.BlockSpec(pos.shape, lambda b, gi: (0, 0, 0, 0)),
                  pl.BlockSpec(w1.shape, lambda b, gi: (0, 0, 0, 0)),
                  pl.BlockSpec(w2.shape, lambda b, gi: (0, 0))],
        out_specs=[spec, spec],
        out_shape=[shape, shape],
        compiler_params=_cparams("parallel", "parallel"),
        name="nsa_compress",
    )(kvc, pos, w1, w2)


def _softmax_pv(st, vt, m, acc):
    m_new = jnp.maximum(m, jnp.max(st, axis=0, keepdims=True))
    p = jnp.exp2(st - m_new).astype(BF16)
    return m_new, jnp.exp2(m - m_new) * acc + _dot(vt, p)


def _select_bias(psum, ov_ref, q0, tq):
    nsb = ov_ref.shape[0]
    hi, lo = _split_bf16(psum, 2)
    pslc = _dot(ov_ref[...], hi) + _dot(ov_ref[...], lo)
    j_io = lax.broadcasted_iota(jnp.int32, (nsb, tq), 0)
    pos = q0 + lax.broadcasted_iota(jnp.int32, (nsb, tq), 1)
    cur = jnp.right_shift(pos, SEL_BLOCK.bit_length() - 1)
    allowed = j_io * SEL_BLOCK <= pos
    forced = (j_io == 0) | ((j_io <= cur) & (j_io > cur - N_LOCAL_BLOCKS))
    score = jnp.where(forced, SEL_FORCE, jnp.where(allowed, pslc, NEG_INF))
    rank = jnp.zeros((nsb, tq), F32)
    for i in range(nsb):
        row = score[i:i + 1, :]
        before = (row > score) | ((row == score) & (j_io > i))
        rank = rank + jnp.where(before, 1.0, 0.0)
    return jnp.where(rank < float(min(N_SELECT, nsb)), 0.0, -MASK_BIG)


def _nsa_attn_kernel(qt_ref, ks_ref, vst_ref, kw_ref, vwt_ref, kc_ref, vct_ref, gate_ref, ov_ref, pot_ref, o_ref):
    hpg, hd = NSA_HPG, NSA_HEAD_DIM
    tq = qt_ref.shape[4]
    tk = ATT_TK
    i = pl.program_id(2)
    q0 = i * tq
    cols = hpg * tq
    qt = [qt_ref[0, 0, h] for h in range(hpg)]
    qt_raw = jnp.concatenate(qt, axis=1)
    d = (lax.broadcasted_iota(jnp.int32, (tk, cols), 0)
         - (lax.broadcasted_iota(jnp.int32, (tk, cols), 1) & (tq - 1)))

    def ktile(ref, j):
        return ref[0, 0, pl.ds(pl.multiple_of(j * tk, tk), tk), :]

    n_wt = WINDOW // tk + 1
    j0 = jnp.maximum(i - (n_wt - 1), 0)
    st_w = []
    for t in range(n_wt):
        off = (j0 + t - i) * tk
        st = jnp.where(d <= -off, _dot(ktile(kw_ref, j0 + t), qt_raw), NEG_INF)
        if t == 0:
            st = jnp.where(d > -WINDOW - off, st, NEG_INF)
        st_w.append(st)
    m_w = functools.reduce(jnp.maximum, [jnp.max(st, axis=0, keepdims=True) for st in st_w])
    acc = sum(_dot(vwt_ref[0, 0, j0 + t], jnp.exp2(st_w[t] - m_w).astype(BF16)) for t in range(n_wt))
    o_win = acc * (1.0 / acc[hd:hd + 1, :])

    kc = kc_ref[0, 0]
    vct = vct_ref[0, 0]
    c_io = lax.broadcasted_iota(jnp.int32, (CMP_ROWS, tq), 0)
    pos_c = q0 + lax.broadcasted_iota(jnp.int32, (CMP_ROWS, tq), 1)
    cmask = c_io * COMP_STRIDE + (COMP_BLOCK - 1) <= pos_c
    psum = jnp.zeros((CMP_ROWS, tq), F32)
    o_cmp = []
    for h in range(hpg):
        st = jnp.where(cmask, _dot(kc, qt[h]), NEG_INF)
        e = jnp.where(cmask, jnp.exp2(st - jnp.max(st, axis=0, keepdims=True)), 0.0)
        den = jnp.sum(e, axis=0, keepdims=True)
        pc = e / jnp.where(den > 0.0, den, 1.0)
        psum = psum + pc
        o_cmp.append(_dot(vct, pc.astype(BF16)))

    nsb = ov_ref.shape[0]
    bias = lax.cond(q0 >= N_SELECT * SEL_BLOCK,
                    lambda: _select_bias(psum, ov_ref, q0, tq),
                    lambda: jnp.zeros((nsb, tq), F32)).astype(BF16)
    qt_sel = jnp.concatenate([jnp.concatenate([qt[h][:hd], bias, qt[h][hd + nsb:]], axis=0) for h in range(hpg)],
                             axis=1)

    def sel_body(j, carry):
        st, m, acc = carry
        st_next = _dot(ktile(ks_ref, j + 1), qt_sel)
        m, acc = _softmax_pv(st, vst_ref[0, 0, j], m, acc)
        return st_next, m, acc

    init = (_dot(ktile(ks_ref, 0), qt_sel), jnp.full((1, cols), NEG_INF, F32), jnp.zeros((LANES, cols), F32))
    st, m, acc = lax.fori_loop(0, i, sel_body, init)
    st = jnp.where(d <= 0, st, NEG_INF)
    _, acc = _softmax_pv(st, vst_ref[0, 0, i], m, acc)
    o_sel = acc * (1.0 / acc[hd:hd + 1, :])

    gt = _sigmoid(gate_ref[...]).T
    out = jnp.zeros((hpg * hd, tq), F32)
    for h in range(hpg):
        gb = h * N_BRANCHES
        cs = slice(h * tq, (h + 1) * tq)
        o_h = (gt[gb:gb + 1, :] * o_cmp[h]
               + gt[gb + 1:gb + 2, :] * o_sel[:, cs]
               + gt[gb + 2:gb + 3, :] * o_win[:, cs])
        out = out + _dot(pot_ref[h], o_h.astype(BF16))
    o_ref[...] = out.T.astype(BF16)


def _nsa_attn(qt, ks, vst, kw, vwt, kc, vct, gates, ov_t, place_ot, batch, seq_len):
    t = batch * seq_len
    tq, tk = ATT_TQ, ATT_TK
    nq = seq_len // tq
    g, hpg, hd = NSA_KV_GROUPS, NSA_HPG, NSA_HEAD_DIM
    kspec = pl.BlockSpec((1, 1, seq_len, LANES), lambda b, gi, i: (b, gi, 0, 0))
    vtspec = pl.BlockSpec((1, 1, seq_len // tk, LANES, tk), lambda b, gi, i: (b, gi, 0, 0, 0))
    cspec = pl.BlockSpec((1, 1, CMP_ROWS, LANES), lambda b, gi, i: (b, gi, 0, 0))
    return pl.pallas_call(
        _nsa_attn_kernel,
        grid=(batch, g, nq),
        in_specs=[pl.BlockSpec((1, 1, hpg, LANES, tq), lambda b, gi, i: (b, gi, 0, 0, i)),
                  kspec, vtspec, kspec, vtspec, cspec, cspec,
                  pl.BlockSpec((tq, LANES), lambda b, gi, i: (b * nq + i, gi)),
                  pl.BlockSpec(ov_t.shape, lambda b, gi, i: (0, 0)),
                  pl.BlockSpec(place_ot.shape, lambda b, gi, i: (0, 0, 0))],
        out_specs=pl.BlockSpec((tq, hpg * hd), lambda b, gi, i: (b * nq + i, gi)),
        out_shape=jax.ShapeDtypeStruct((t, g * hpg * hd), BF16),
        compiler_params=_cparams("parallel", "parallel", "arbitrary"),
        name="nsa_attention",
    )(qt, ks, vst, kw, vwt, kc, vct, gates, ov_t, place_ot)


def _nsa_constants(seq_len):
    nsb = seq_len // SEL_BLOCK
    nc = CMP_ROWS
    cs = np.arange(nc) * COMP_STRIDE
    ss = np.arange(nsb) * SEL_BLOCK
    ov = (np.minimum(cs[:, None] + COMP_BLOCK, ss[None, :] + SEL_BLOCK)
          - np.maximum(cs[:, None], ss[None, :]))
    ov = np.clip(ov, 0, None).astype(np.float32) / COMP_BLOCK
    ov[(seq_len - COMP_BLOCK) // COMP_STRIDE + 1:, :] = 0.0
    ov_t = jnp.asarray(ov.T, BF16)
    place_ot = np.zeros((NSA_HPG, NSA_HPG * NSA_HEAD_DIM, LANES), np.float32)
    for h in range(NSA_HPG):
        place_ot[h, h * NSA_HEAD_DIM + np.arange(NSA_HEAD_DIM), np.arange(NSA_HEAD_DIM)] = 1.0
    onehot = np.zeros((seq_len, LANES), np.float32)
    onehot[np.arange(seq_len), NSA_HEAD_DIM + np.arange(seq_len) // SEL_BLOCK] = 1.0
    return ov_t, jnp.asarray(place_ot, BF16), jnp.asarray(onehot, F32)


def _prep_nsa_weights(w_in, cmp_pos, cmp_w1, cmp_w2, w_out):
    d, g, hpg, hd = D_MODEL, NSA_KV_GROUPS, NSA_HPG, NSA_HEAD_DIM
    kv_end = NSA_Q_DIM + 2 * N_BRANCHES * NSA_KV_DIM
    half = COMP_BLOCK // 2
    hid = COMP_HIDDEN

    def lane_slots(w, n):
        return jnp.pad(w.reshape(d, n, hd), ((0, 0), (0, 0), (0, LANES - hd))).reshape(d, n * LANES)

    scale = hd ** -0.5 * np.log2(np.e)
    wq = lane_slots(w_in[:, :NSA_Q_DIM] * scale, NSA_HEADS)
    kv = w_in[:, NSA_Q_DIM:kv_end].reshape(d, 2 * N_BRANCHES, g, hd)
    w_sw = [lane_slots(kv[:, br].reshape(d, g * hd), g) for br in (2, 3, 4, 5)]
    w_kvc = jnp.concatenate([kv[:, 0], kv[:, 1]], axis=-1).reshape(d, g * LANES)
    wg = w_in[:, kv_end:].reshape(d, g, hpg * N_BRANCHES)
    wg = jnp.pad(wg, ((0, 0), (0, 0), (0, LANES - hpg * N_BRANCHES))).reshape(d, g * LANES)
    w_all = jnp.concatenate([wq] + w_sw + [w_kvc, wg], axis=1).astype(BF16)

    w1 = cmp_w1.reshape(2, 2, half, hd, hid)
    zeros = jnp.zeros_like(w1[0])
    w1bd = jnp.concatenate([jnp.concatenate([w1[0], zeros], axis=-1),
                            jnp.concatenate([zeros, w1[1]], axis=-1)], axis=-2).astype(BF16)
    pos = cmp_pos.reshape(2, 2, half, 1, hd)
    posbd = jnp.concatenate([pos[0], pos[1]], axis=-1)
    w2bd = jnp.zeros((2 * hid, 2 * LANES), F32)
    w2bd = w2bd.at[:hid, :hd].set(cmp_w2[0]).at[hid:, LANES:LANES + hd].set(cmp_w2[1]).astype(BF16)
    return w_all, posbd, w1bd, w2bd, w_out.astype(BF16)


def _nsa_layer(h, g_pre, g_post, weights, consts, batch, seq_len):
    w_all, posbd, w1bd, w2bd, w_out = weights
    ov_t, place_ot, onehot = consts
    qt, ks, vst, kw, vwt, kvc, gates = _nsa_in(h, g_pre, w_all, onehot, batch, seq_len)
    kc, vct = _compress(kvc, posbd, w1bd, w2bd, batch, seq_len)
    o = _nsa_attn(qt, ks, vst, kw, vwt, kc, vct, gates, ov_t, place_ot, batch, seq_len)
    return _out_proj(o, w_out, g_post, h)


def _ssd_in_kernel(x_ref, g_ref, w_ref, cw_ref, dtb_ref, z_ref, xbc_ref, dt_ref, xe_ref, *, tiles_per_seq):
    tm = x_ref.shape[0]
    xn = _rms(x_ref[...], g_ref[...]).astype(BF16)
    _fill_halo(xe_ref, xn, tm, pl.program_id(0) % tiles_per_seq == 0)
    z_ref[...] = _dot(xn, w_ref[:, :SSD_D_INNER]).astype(z_ref.dtype)
    c0, c1 = SSD_D_INNER, SSD_D_INNER + SSD_CONV_DIM
    u = _causal_conv(_dot(xe_ref[...], w_ref[:, c0:c1]), cw_ref[...], SSD_CONV)
    xbc_ref[...] = (u * _sigmoid(u)).astype(xbc_ref.dtype)
    dt_raw = _dot(xn, w_ref[:, c1:]) + dtb_ref[...]
    dt_ref[...] = jnp.maximum(dt_raw, 0.0) + jnp.log1p(jnp.exp(-jnp.abs(dt_raw)))


def _ssd_in(x, gain, w, cw, dtb, seq_len):
    t, d = x.shape
    tm = PROJ_TM
    kern = functools.partial(_ssd_in_kernel, tiles_per_seq=seq_len // tm)
    return pl.pallas_call(
        kern,
        grid=(t // tm,),
        in_specs=[pl.BlockSpec((tm, d), lambda i: (i, 0)),
                  _resident((1, d)), _resident(w.shape), _resident(cw.shape), _resident(dtb.shape)],
        out_specs=[pl.BlockSpec((tm, SSD_D_INNER), lambda i: (i, 0)),
                   pl.BlockSpec((tm, SSD_CONV_DIM), lambda i: (i, 0)),
                   pl.BlockSpec((tm, LANES), lambda i: (i, 0))],
        out_shape=[jax.ShapeDtypeStruct((t, SSD_D_INNER), BF16),
                   jax.ShapeDtypeStruct((t, SSD_CONV_DIM), BF16),
                   jax.ShapeDtypeStruct((t, LANES), F32)],
        scratch_shapes=[pltpu.VMEM((tm + HALO, d), BF16)],
        compiler_params=_cparams("arbitrary"),
        name="ssd_in_proj",
    )(x, gain, w, cw, dtb)


def _ssd_scan_kernel(xbc_ref, z_ref, dt_ref, alog_ref, dskip_ref, nw_ref, tri_ref, y_ref, state_ref):
    L = SSD_CHUNK
    hd = SSD_HEAD_DIM
    gw = SSD_GROUP_WIDTH
    n = SSD_STATE

    @pl.when(pl.program_id(1) == 0)
    def _():
        state_ref[...] = jnp.zeros(state_ref.shape, F32)

    dt = dt_ref[...]
    dta = dt * (-jnp.exp(alog_ref[...]))
    tri = tri_ref[...]
    acs = sum(_dot(tri, part) for part in _split_bf16(dta, 3))
    acs_t = acs.T
    dt_t = dt.T
    last = acs[L - 1:L, :]
    e_in = jnp.exp(acs)
    dec = jnp.exp(last - acs) * dt
    e_last = jnp.exp(last)
    causal = (lax.broadcasted_iota(jnp.int32, (L, L), 0) >= lax.broadcasted_iota(jnp.int32, (L, L), 1))
    lo_half = lax.broadcasted_iota(jnp.int32, (L, LANES), 1) < hd
    lo_row = lo_half[0:1, :]
    zero_bf = jnp.zeros((L, LANES), BF16)

    for g in range(SSD_GROUPS):
        b_g = xbc_ref[:, SSD_D_INNER + g * n:SSD_D_INNER + (g + 1) * n]
        c_g = xbc_ref[:, SSD_D_INNER + (SSD_GROUPS + g) * n:SSD_D_INNER + (SSD_GROUPS + g + 1) * n]
        cb = _dot_nt(c_g, b_g)
        b_t = b_g.astype(F32).T.astype(BF16)
        st = state_ref[g]
        y_inter = _dot(c_g, st.astype(BF16))
        x_g = xbc_ref[:, g * gw:(g + 1) * gw]
        ys, xds, els = [], [], []
        for pr in range(SSD_HEADS_PER_GROUP // 2):
            heads = (g * SSD_HEADS_PER_GROUP + 2 * pr, g * SSD_HEADS_PER_GROUP + 2 * pr + 1)
            xp = x_g[:, pr * LANES:(pr + 1) * LANES]
            ws = []
            for h in heads:
                seg = acs[:, h:h + 1] - acs_t[h:h + 1, :]
                lm = jnp.exp(jnp.where(causal, seg, NEG_INF))
                ws.append((cb * lm * dt_t[h:h + 1, :]).astype(BF16))
            w2 = jnp.concatenate(ws, axis=1)
            x2 = jnp.concatenate([jnp.where(lo_half, xp, zero_bf), jnp.where(lo_half, zero_bf, xp)], axis=0)
            h0, h1 = heads
            e_p = jnp.where(lo_half, e_in[:, h0:h0 + 1], e_in[:, h1:h1 + 1])
            d_p = jnp.where(lo_half, dec[:, h0:h0 + 1], dec[:, h1:h1 + 1])
            ys.append(_dot(w2, x2) + y_inter[:, pr * LANES:(pr + 1) * LANES] * e_p)
            xds.append((xp.astype(F32) * d_p).astype(BF16))
            els.append(jnp.where(lo_row, e_last[:, h0:h0 + 1], e_last[:, h1:h1 + 1]))
        xd = jnp.concatenate(xds, axis=1)
        state_ref[g] = st * jnp.concatenate(els, axis=1) + _dot(b_t, xd)
        sl = slice(g * gw, (g + 1) * gw)
        yg = jnp.concatenate(ys, axis=1) + dskip_ref[:, sl] * x_g.astype(F32)
        zg = z_ref[:, sl].astype(F32)
        yg = yg * (zg * _sigmoid(zg))
        y_ref[:, sl] = _rms(yg, nw_ref[:, sl]).astype(y_ref.dtype)


def _ssd_scan(xbc, z, dt, alog, dskip, nw, tri, batch, seq_len):
    t = xbc.shape[0]
    L = SSD_CHUNK
    nc = seq_len // L
    row = lambda b, c: (b * nc + c, 0)
    return pl.pallas_call(
        _ssd_scan_kernel,
        grid=(batch, nc),
        in_specs=[pl.BlockSpec((L, SSD_CONV_DIM), row),
                  pl.BlockSpec((L, SSD_D_INNER), row),
                  pl.BlockSpec((L, LANES), row),
                  pl.BlockSpec((1, LANES), lambda b, c: (0, 0)),
                  pl.BlockSpec((1, SSD_D_INNER), lambda b, c: (0, 0)),
                  pl.BlockSpec((1, SSD_D_INNER), lambda b, c: (0, 0)),
                  pl.BlockSpec((L, L), lambda b, c: (0, 0))],
        out_specs=pl.BlockSpec((L, SSD_D_INNER), row),
        out_shape=jax.ShapeDtypeStruct((t, SSD_D_INNER), BF16),
        scratch_shapes=[pltpu.VMEM((SSD_GROUPS, SSD_STATE, SSD_GROUP_WIDTH), F32)],
        compiler_params=_cparams("parallel", "arbitrary"),
        name="ssd_scan",
    )(xbc, z, dt, alog, dskip, nw, tri)


def _prep_ssd_weights(w_in, conv_w, conv_b, dt_bias, a_log, d_skip, norm_w, w_out):
    pad = LANES - SSD_HEADS
    w_all = jnp.pad(w_in, ((0, 0), (0, pad))).astype(BF16)
    cw = jnp.concatenate([conv_w, conv_b[None, :], jnp.zeros((8 - SSD_CONV - 1, SSD_CONV_DIM), F32)], axis=0)
    dtb = jnp.pad(dt_bias, (0, pad))[None, :]
    alog = jnp.pad(a_log, (0, pad))[None, :]
    dskip = jnp.repeat(d_skip, SSD_HEAD_DIM)[None, :]
    return w_all, cw, dtb, alog, dskip, norm_w[None, :], w_out.astype(BF16)


def _ssd_layer(h, g_pre, g_post, weights, tri, batch, seq_len):
    w_all, cw, dtb, alog, dskip, nw, w_out = weights
    z, xbc, dt = _ssd_in(h, g_pre, w_all, cw, dtb, seq_len)
    y = _ssd_scan(xbc, z, dt, alog, dskip, nw, tri, batch, seq_len)
    return _out_proj(y, w_out, g_post, h)


def kernel(x, norm_gains, nsa_w_in, nsa_cmp_pos, nsa_cmp_w1, nsa_cmp_w2, nsa_w_out, ssd_w_in, ssd_conv_w, ssd_conv_b, ssd_dt_bias, ssd_a_log, ssd_d, ssd_norm_w, ssd_w_out, ffn_w_up, ffn_conv_w, ffn_conv_b, ffn_w_down):
    batch, seq_len, d = x.shape
    assert d == D_MODEL and seq_len % FFN_TM == 0 and seq_len % ATT_TQ == 0 and seq_len % SSD_CHUNK == 0
    assert seq_len // (COMP_BLOCK // 2) == CMP_ROWS and seq_len % PROJ_TM == 0 and seq_len >= WINDOW + ATT_TK
    assert ATT_TQ == ATT_TK == PROJ_TM and WINDOW % ATT_TK == 0
    assert (N_SELECT * SEL_BLOCK) % ATT_TQ == 0 and (seq_len // SEL_BLOCK) % HALO == 0
    assert NSA_HEAD_DIM + seq_len // SEL_BLOCK <= LANES
    h = x.reshape(batch * seq_len, d)
    nsa_consts = _nsa_constants(seq_len)
    tri = jnp.asarray(np.tril(np.ones((SSD_CHUNK, SSD_CHUNK), np.float32)), BF16)
    for i in range(DEPTH):
        gains = norm_gains[i][:, None, :]
        slot = i // N_MIXERS
        if i % N_MIXERS == 0:
            w = _prep_nsa_weights(nsa_w_in[slot], nsa_cmp_pos[slot], nsa_cmp_w1[slot],
                                  nsa_cmp_w2[slot], nsa_w_out[slot])
            h = _nsa_layer(h, gains[0], gains[1], w, nsa_consts, batch, seq_len)
        else:
            w = _prep_ssd_weights(ssd_w_in[slot], ssd_conv_w[slot], ssd_conv_b[slot], ssd_dt_bias[slot],
                                  ssd_a_log[slot], ssd_d[slot], ssd_norm_w[slot], ssd_w_out[slot])
            h = _ssd_layer(h, gains[0], gains[1], w, tri, batch, seq_len)
        fw = _prep_ffn_weights(ffn_w_up[i], ffn_conv_w[i], ffn_conv_b[i], ffn_w_down[i])
        h = _ffn(h, gains[2], *fw, gains[3], seq_len)
    return h.reshape(batch, seq_len, d)
```
